```python
import jax, jax.numpy as jnp
from jax import lax
import numpy as np

D_MODEL = 1024
BATCH = 8
SEQ = 4096
DEPTH = 2

N_Q_HEADS = 16
N_KV_HEADS = 4
HEAD_DIM = 64
GROUP = N_Q_HEADS // N_KV_HEADS
WINDOW = 128
BLOCK = 128
ATTN_W = N_Q_HEADS * HEAD_DIM
KV_W = N_KV_HEADS * HEAD_DIM
CONV_W = D_MODEL
CONV_K = 3
REC_W = D_MODEL
REC_HEADS = 4
REC_HEAD_DIM = REC_W // REC_HEADS
REC_CONV_K = 4
LRU_C = 8.0
N_BRANCH = 3
BRANCH_W = D_MODEL
SPLIT_SIZES = (ATTN_W, KV_W, KV_W, CONV_W, CONV_W, CONV_W, REC_W, REC_W, N_BRANCH * D_MODEL)
IN_COLS = ATTN_W + 2 * KV_W + 3 * CONV_W + 2 * REC_W + N_BRANCH * D_MODEL
N_KEYS = 128
N_EXPERTS = N_KEYS * N_KEYS
PEER_HEADS = 8
PEER_TOPK = 16
PEER_KEY_DIM = 256
PEER_HALF = PEER_KEY_DIM // 2
PEER_CHUNK = 128
PLE_DIM = 256
EPS = 1e-6
NEG_INF = -1e30

kernel_name = "hybrid_swa_conv_rglru_peer"


def rmsnorm(x, g):
    x32 = x.astype(jnp.float32)
    y = x32 * lax.rsqrt(jnp.mean(x32 * x32, axis=-1, keepdims=True) + EPS) * g.astype(jnp.float32)
    return y.astype(x.dtype)


def causal_depthwise_conv(x, w, b=None):
    k_width, ch = w.shape
    y = lax.conv_general_dilated(
        x, w[:, None, :].astype(x.dtype), window_strides=(1,), padding=[(k_width - 1, 0)],
        dimension_numbers=('NWC', 'WIO', 'NWC'), feature_group_count=ch)
    if b is not None:
        y = y + b.astype(y.dtype)
    return y


def sliding_window_attention(q, k, v, sinks):
    b, s = q.shape[0], q.shape[1]
    nb = s // BLOCK
    qb = q.reshape(b, nb, BLOCK, N_KV_HEADS, GROUP, HEAD_DIM)

    def band(t):
        tp = jnp.pad(t, ((0, 0), (BLOCK, 0), (0, 0), (0, 0)))
        prev = tp[:, :s].reshape(b, nb, BLOCK, N_KV_HEADS, HEAD_DIM)
        cur = t.reshape(b, nb, BLOCK, N_KV_HEADS, HEAD_DIM)
        return jnp.concatenate([prev, cur], axis=2)

    kb, vb = band(k), band(v)
    scores = jnp.einsum('bnqhgd,bnkhd->bnhgqk', qb, kb,
                        preferred_element_type=jnp.float32) * (HEAD_DIM ** -0.5)
    q_pos = jnp.arange(nb)[:, None, None] * BLOCK + jnp.arange(BLOCK)[None, :, None]
    k_pos = jnp.arange(nb)[:, None, None] * BLOCK - BLOCK + jnp.arange(2 * BLOCK)[None, None, :]
    diff = q_pos - k_pos
    mask = (diff >= 0) & (diff < WINDOW) & (k_pos >= 0)
    scores = jnp.where(mask[None, :, None, None], scores, NEG_INF)
    sink = sinks.astype(jnp.float32).reshape(1, 1, N_KV_HEADS, GROUP, 1, 1)
    m = jnp.maximum(scores.max(axis=-1, keepdims=True), sink)
    e = jnp.exp(scores - m)
    probs = e / (e.sum(axis=-1, keepdims=True) + jnp.exp(sink - m))
    out = jnp.einsum('bnhgqk,bnkhd->bnqhgd', probs.astype(v.dtype), vb)
    return out.reshape(b, s, ATTN_W)


def rg_lru(x, w_r, b_r, w_i, b_i, lam):
    b, s, _ = x.shape
    x32 = x.astype(jnp.float32)
    xh = x32.reshape(b, s, REC_HEADS, REC_HEAD_DIM)
    r = jax.nn.sigmoid(jnp.einsum('bshi,hij->bshj', xh, w_r.astype(jnp.float32)).reshape(b, s, REC_W)
                       + b_r.astype(jnp.float32))
    i = jax.nn.sigmoid(jnp.einsum('bshi,hij->bshj', xh, w_i.astype(jnp.float32)).reshape(b, s, REC_W)
                       + b_i.astype(jnp.float32))
    log_a = -LRU_C * r * jax.nn.softplus(-lam.astype(jnp.float32))
    a = jnp.exp(log_a)
    inp = jnp.sqrt(-jnp.expm1(2.0 * log_a)) * (i * x32)

    def combine(left, right):
        a1, b1 = left
        a2, b2 = right
        return a1 * a2, a2 * b1 + b2

    _, h = lax.associative_scan(combine, (a, inp), axis=1)
    return h.astype(x.dtype)


def peer(xn, w_q, sub_keys, u, v):
    b, s, d = xn.shape
    t = b * s
    xt = xn.reshape(t, d)
    q = (xt @ w_q).astype(jnp.float32).reshape(t, PEER_HEADS, 2, PEER_HALF)
    sc = jnp.einsum('thpc,pnc->thpn', q, sub_keys.astype(jnp.float32))
    top_s, top_i = lax.top_k(sc, PEER_TOPK)
    cand = top_s[:, :, 0, :, None] + top_s[:, :, 1, None, :]
    best_s, best_c = lax.top_k(cand.reshape(t, PEER_HEADS, PEER_TOPK * PEER_TOPK), PEER_TOPK)
    i1 = jnp.take_along_axis(top_i[:, :, 0], best_c // PEER_TOPK, axis=-1)
    i2 = jnp.take_along_axis(top_i[:, :, 1], best_c % PEER_TOPK, axis=-1)
    expert = i1 * N_KEYS + i2
    gate = jax.nn.softmax(best_s, axis=-1)
    nc = t // PEER_CHUNK

    def chunk_fn(args):
        xc, ec, gc = args
        act = jax.nn.gelu(jnp.einsum('cd,chkd->chk', xc, u[ec]))
        return jnp.einsum('chk,chkd->cd', (gc * act).astype(xc.dtype), v[ec])

    out = lax.map(chunk_fn, (xt.reshape(nc, PEER_CHUNK, d),
                             expert.reshape(nc, PEER_CHUNK, PEER_HEADS, PEER_TOPK),
                             gate.reshape(nc, PEER_CHUNK, PEER_HEADS, PEER_TOPK)))
    return out.reshape(b, s, d)


def setup_inputs(seed: int = 0) -> dict:
    key = jax.random.key(seed)
    ks = jax.random.split(key, 26)
    f32 = jnp.float32

    def nrm(k, shape, scale):
        return jax.random.normal(k, shape, f32) * scale

    a0 = jax.random.uniform(ks[12], (DEPTH, REC_W), f32, 0.9, 0.999) ** (1.0 / LRU_C)
    return {
        "x": nrm(ks[0], (BATCH, SEQ, D_MODEL), 1.0),
        "p": nrm(ks[1], (DEPTH, BATCH, SEQ, PLE_DIM), 1.0),
        "norm_mix": 1.0 + nrm(ks[2], (DEPTH, D_MODEL), 0.02),
        "w_in": nrm(ks[3], (DEPTH, D_MODEL, IN_COLS), D_MODEL ** -0.5),
        "attn_sinks": nrm(ks[4], (DEPTH, N_Q_HEADS), 0.5),
        "conv_w": nrm(ks[5], (DEPTH, CONV_K, CONV_W), CONV_K ** -0.5),
        "rec_conv_w": nrm(ks[6], (DEPTH, REC_CONV_K, REC_W), REC_CONV_K ** -0.5),
        "rec_conv_b": nrm(ks[7], (DEPTH, REC_W), 0.02),
        "w_rgate": nrm(ks[8], (DEPTH, REC_HEADS, REC_HEAD_DIM, REC_HEAD_DIM), REC_HEAD_DIM ** -0.5),
        "b_rgate": nrm(ks[9], (DEPTH, REC_W), 0.02),
        "w_igate": nrm(ks[10], (DEPTH, REC_HEADS, REC_HEAD_DIM, REC_HEAD_DIM), REC_HEAD_DIM ** -0.5),
        "b_igate": nrm(ks[11], (DEPTH, REC_W), 0.02),
        "lru_lambda": jnp.log(a0) - jnp.log1p(-a0),
        "w_branch": nrm(ks[13], (DEPTH, N_BRANCH, BRANCH_W, D_MODEL), BRANCH_W ** -0.5),
        "w_out": nrm(ks[14], (DEPTH, D_MODEL, D_MODEL), D_MODEL ** -0.5),
        "norm_ffn": 1.0 + nrm(ks[15], (DEPTH, D_MODEL), 0.02),
        "w_peer_q": nrm(ks[16], (DEPTH, D_MODEL, PEER_HEADS * PEER_KEY_DIM), D_MODEL ** -0.5),
        "peer_sub_keys": nrm(ks[17], (DEPTH, 2, N_KEYS, PEER_HALF), PEER_HALF ** -0.5),
        "peer_u": nrm(ks[18], (DEPTH, N_EXPERTS, D_MODEL), D_MODEL ** -0.5),
        "peer_v": nrm(ks[19], (DEPTH, N_EXPERTS, D_MODEL), (PEER_HEADS * PEER_TOPK) ** -0.5),
        "norm_ple": 1.0 + nrm(ks[20], (DEPTH, D_MODEL), 0.02),
        "w_ple_gate": nrm(ks[21], (DEPTH, D_MODEL, D_MODEL), D_MODEL ** -0.5),
        "w_ple_proj": nrm(ks[22], (DEPTH, PLE_DIM, D_MODEL), PLE_DIM ** -0.5),
        "norm_final": 1.0 + nrm(ks[23], (D_MODEL,), 0.02),
    }


def reference(x, p, norm_mix, w_in, attn_sinks, conv_w, rec_conv_w, rec_conv_b, w_rgate, b_rgate,
              w_igate, b_igate, lru_lambda, w_branch, w_out, norm_ffn, w_peer_q, peer_sub_keys,
              peer_u, peer_v, norm_ple, w_ple_gate, w_ple_proj, norm_final):
    b, s, d = x.shape
    split_points = np.cumsum(SPLIT_SIZES)[:-1].tolist()
    h = x
    for l in range(DEPTH):
        xn = rmsnorm(h, norm_mix[l])
        z = xn @ w_in[l]
        q, k, v, c_b, c_c, c_x, r_x, r_y, gates = jnp.split(z, split_points, axis=-1)
        attn = sliding_window_attention(q.reshape(b, s, N_Q_HEADS, HEAD_DIM),
                                        k.reshape(b, s, N_KV_HEADS, HEAD_DIM),
                                        v.reshape(b, s, N_KV_HEADS, HEAD_DIM), attn_sinks[l])
        conv = c_b * causal_depthwise_conv(c_c * c_x, conv_w[l])
        rec = jax.nn.gelu(r_y) * rg_lru(causal_depthwise_conv(r_x, rec_conv_w[l], rec_conv_b[l]),
                                        w_rgate[l], b_rgate[l], w_igate[l], b_igate[l], lru_lambda[l])
        branches = jnp.stack([attn, conv, rec], axis=0)
        y = jnp.einsum('nbsw,nwd->nbsd', branches, w_branch[l])
        g = jax.nn.sigmoid(gates.reshape(b, s, N_BRANCH, d))
        merged = jnp.einsum('bsnd,nbsd->bsd', g, y)
        h = h + merged @ w_out[l]
        h = h + peer(rmsnorm(h, norm_ffn[l]), w_peer_q[l], peer_sub_keys[l], peer_u[l], peer_v[l])
        ple_gate = jax.nn.sigmoid(rmsnorm(h, norm_ple[l]) @ w_ple_gate[l])
        h = h + ple_gate * (p[l] @ w_ple_proj[l])
    return rmsnorm(h, norm_final)
```

```python
import functools

import jax
import jax.numpy as jnp
from jax import lax
from jax.experimental import pallas as pl
from jax.experimental.pallas import tpu as pltpu

F32 = jnp.float32
BF16 = jnp.bfloat16

D_MODEL = 1024
N_Q_HEADS = 16
N_KV_HEADS = 4
HEAD_DIM = 64
GROUP = N_Q_HEADS // N_KV_HEADS
ATTN_BLOCK = 128
KV_W = N_KV_HEADS * HEAD_DIM
REC_HEADS = 4
REC_HEAD_DIM = D_MODEL // REC_HEADS
LRU_C = 8.0
N_KEYS = 128
N_EXPERTS = N_KEYS * N_KEYS
PEER_HEADS = 8
PEER_TOPK = 16
PEER_HALF = 128
N_PAIRS = PEER_HEADS * PEER_TOPK
PLE_DIM = 256
EPS = 1e-6
NEG_INF = -1e30

COL_Q, COL_CB, COL_CC, COL_CX, COL_RX, COL_RY, COL_G0 = 0, 1, 2, 3, 4, 5, 6
IN_COLS = 9 * D_MODEL + 2 * KV_W
COL_K_KV = 9 * D_MODEL // KV_W
COL_V_KV = COL_K_KV + 1

LANES = 128
SLAB = D_MODEL // 2 // LANES
ROW_WORDS = SLAB * LANES
TILE_STRIDE = N_PAIRS + 8
TOKENS_IN_FLIGHT = 4
VMEM_LIMIT = 56 * 1024 * 1024


def _params(sem, vmem=None):
    return pltpu.CompilerParams(dimension_semantics=sem, vmem_limit_bytes=vmem)


def _rms(x32, g):
    ms = jnp.mean(x32 * x32, axis=-1, keepdims=True)
    return x32 * lax.rsqrt(ms + EPS) * g


def _sigmoid(x):
    return 1.0 / (1.0 + jnp.exp(-x))


def _gelu(x):
    return 0.5 * x * (1.0 + jnp.tanh(0.7978845608028654 * (x + 0.044715 * (x * x * x))))


def _split_bf16(x32):
    hi = x32.astype(BF16)
    lo = (x32 - hi.astype(F32)).astype(BF16)
    return hi, lo


def _norm_matmul_kernel(x_ref, g_ref, w_ref, o_ref, xn_ref):
    @pl.when(pl.program_id(1) == 0)
    def _():
        xn_ref[...] = _rms(x_ref[...], g_ref[...]).astype(BF16)

    o_ref[...] = jnp.dot(xn_ref[...], w_ref[...],
                         preferred_element_type=F32).astype(o_ref.dtype)


def norm_matmul(x, g, w, tm=2048, tn=512):
    m, d = x.shape
    n = w.shape[1]
    tm = min(tm, m)
    return pl.pallas_call(
        _norm_matmul_kernel,
        grid=(m // tm, n // tn),
        in_specs=[pl.BlockSpec((tm, d), lambda i, j: (i, 0)),
                  pl.BlockSpec((1, d), lambda i, j: (0, 0)),
                  pl.BlockSpec((d, tn), lambda i, j: (0, j))],
        out_specs=pl.BlockSpec((tm, tn), lambda i, j: (i, j)),
        out_shape=jax.ShapeDtypeStruct((m, n), BF16),
        scratch_shapes=[pltpu.VMEM((tm, d), BF16)],
        compiler_params=_params(("arbitrary", "arbitrary"), VMEM_LIMIT),
        name="norm_matmul",
    )(x, g, w)


def _attn_kernel(sink_ref, q_ref, kc_ref, vc_ref, kp_ref, vp_ref, o_ref):
    n = pl.program_id(1)
    q = q_ref[...]
    k = jnp.concatenate([kp_ref[...], kc_ref[...]], axis=0)
    v = jnp.concatenate([vp_ref[...], vc_ref[...]], axis=0)
    shape = (ATTN_BLOCK, 2 * ATTN_BLOCK)
    qi = lax.broadcasted_iota(jnp.int32, shape, 0)
    ki = lax.broadcasted_iota(jnp.int32, shape, 1)
    diff = qi + ATTN_BLOCK - ki
    first = jnp.where(n > 0, 0, ATTN_BLOCK)
    inside = jnp.where(diff >= 0, jnp.where(diff < ATTN_BLOCK, ki - first, -1), -1)
    mask = inside >= 0
    outs = []
    for h in range(N_KV_HEADS):
        kh = k[:, h * HEAD_DIM:(h + 1) * HEAD_DIM]
        vh = v[:, h * HEAD_DIM:(h + 1) * HEAD_DIM]
        for g in range(GROUP):
            hq = h * GROUP + g
            qh = q[:, hq * HEAD_DIM:(hq + 1) * HEAD_DIM]
            s = lax.dot_general(qh, kh, (((1,), (1,)), ((), ())),
                                preferred_element_type=F32) * (HEAD_DIM ** -0.5)
            s = jnp.where(mask, s, NEG_INF)
            sink = sink_ref[hq]
            m = jnp.maximum(jnp.max(s, axis=-1, keepdims=True), sink)
            e = jnp.exp(s - m)
            denom = jnp.sum(e, axis=-1, keepdims=True) + jnp.exp(sink - m)
            p = (e / denom).astype(BF16)
            outs.append(jnp.dot(p, vh, preferred_element_type=F32))
    o_ref[...] = jnp.concatenate(outs, axis=-1).astype(o_ref.dtype)


def attention(z, sinks, batch, seq):
    nb = seq // ATTN_BLOCK
    t = batch * seq
    cur = lambda b, n: b * nb + n
    prev = lambda b, n: b * nb + jnp.maximum(n - 1, 0)
    return pl.pallas_call(
        _attn_kernel,
        grid=(batch, nb),
        in_specs=[pl.BlockSpec(memory_space=pltpu.SMEM),
                  pl.BlockSpec((ATTN_BLOCK, D_MODEL), lambda b, n: (cur(b, n), COL_Q)),
                  pl.BlockSpec((ATTN_BLOCK, KV_W), lambda b, n: (cur(b, n), COL_K_KV)),
                  pl.BlockSpec((ATTN_BLOCK, KV_W), lambda b, n: (cur(b, n), COL_V_KV)),
                  pl.BlockSpec((ATTN_BLOCK, KV_W), lambda b, n: (prev(b, n), COL_K_KV)),
                  pl.BlockSpec((ATTN_BLOCK, KV_W), lambda b, n: (prev(b, n), COL_V_KV))],
        out_specs=pl.BlockSpec((ATTN_BLOCK, D_MODEL), lambda b, n: (cur(b, n), 0)),
        out_shape=jax.ShapeDtypeStruct((t, D_MODEL), BF16),
        compiler_params=_params(("arbitrary", "arbitrary")),
        name="swa_attention",
    )(sinks, z, z, z, z, z)


HALO = 8


def _convrec_kernel(cb_ref, cc_ref, cx_ref, rx_ref, ry_ref, cw_ref, rcw_ref, rcb_ref,
                    wr_ref, wi_ref, br_ref, bi_ref, lam_ref, conv_ref, rec_ref,
                    ubuf, xbuf, hcar):
    ts = cb_ref.shape[0]

    @pl.when(pl.program_id(1) == 0)
    def _():
        ubuf[0:HALO, :] = jnp.zeros((HALO, D_MODEL), F32)
        xbuf[0:HALO, :] = jnp.zeros((HALO, D_MODEL), F32)
        hcar[...] = jnp.zeros_like(hcar)

    u = cc_ref[...].astype(F32) * cx_ref[...].astype(F32)
    ubuf[HALO:HALO + ts, :] = u
    y = (cw_ref[2:3, :] * u + cw_ref[1:2, :] * ubuf[HALO - 1:HALO - 1 + ts, :]
         + cw_ref[0:1, :] * ubuf[HALO - 2:HALO - 2 + ts, :])
    conv_ref[...] = (cb_ref[...].astype(F32) * y).astype(conv_ref.dtype)
    ubuf[0:HALO, :] = ubuf[ts:ts + HALO, :]

    x = rx_ref[...].astype(F32)
    xbuf[HALO:HALO + ts, :] = x
    c = (rcw_ref[3:4, :] * x + rcw_ref[2:3, :] * xbuf[HALO - 1:HALO - 1 + ts, :]
         + rcw_ref[1:2, :] * xbuf[HALO - 2:HALO - 2 + ts, :]
         + rcw_ref[0:1, :] * xbuf[HALO - 3:HALO - 3 + ts, :] + rcb_ref[...])
    xbuf[0:HALO, :] = xbuf[ts:ts + HALO, :]
    cbf = c.astype(BF16)
    r_parts, i_parts = [], []
    for h in range(REC_HEADS):
        ch = cbf[:, h * REC_HEAD_DIM:(h + 1) * REC_HEAD_DIM]
        r_parts.append(jnp.dot(ch, wr_ref[h], preferred_element_type=F32))
        i_parts.append(jnp.dot(ch, wi_ref[h], preferred_element_type=F32))
    r = _sigmoid(jnp.concatenate(r_parts, axis=-1) + br_ref[...])
    ig = _sigmoid(jnp.concatenate(i_parts, axis=-1) + bi_ref[...])
    nl = -lam_ref[...]
    softplus = jnp.maximum(nl, 0.0) + jnp.log(1.0 + jnp.exp(-jnp.abs(nl)))
    log_a = -LRU_C * r * softplus
    a = jnp.exp(log_a)
    b = jnp.sqrt(1.0 - jnp.exp(2.0 * log_a)) * (ig * c)
    d = 1
    while d < ts:
        a_sh = jnp.concatenate([jnp.ones((d, D_MODEL), F32), a[:ts - d]], axis=0)
        b_sh = jnp.concatenate([jnp.zeros((d, D_MODEL), F32), b[:ts - d]], axis=0)
        b = a * b_sh + b
        a = a * a_sh
        d *= 2
    hs = b + a * hcar[0:1, :]
    hcar[0:1, :] = hs[ts - 1:ts, :]
    rec_ref[...] = (_gelu(ry_ref[...].astype(F32)) * hs).astype(rec_ref.dtype)


def conv_rec(z, cw, rcw, rcb, wr, wi, br, bi, lam, batch, seq, ts=256):
    ts = min(ts, seq)
    nt = seq // ts
    t = batch * seq
    row = lambda b, i: b * nt + i
    zspec = lambda col: pl.BlockSpec((ts, D_MODEL), lambda b, i: (row(b, i), col))
    full = lambda a: pl.BlockSpec(a.shape, lambda b, i: (0,) * a.ndim)
    return pl.pallas_call(
        _convrec_kernel,
        grid=(batch, nt),
        in_specs=[zspec(COL_CB), zspec(COL_CC), zspec(COL_CX), zspec(COL_RX), zspec(COL_RY),
                  full(cw), full(rcw), full(rcb), full(wr), full(wi), full(br), full(bi),
                  full(lam)],
        out_specs=[pl.BlockSpec((ts, D_MODEL), lambda b, i: (row(b, i), 0)),
                   pl.BlockSpec((ts, D_MODEL), lambda b, i: (row(b, i), 0))],
        out_shape=[jax.ShapeDtypeStruct((t, D_MODEL), BF16),
                   jax.ShapeDtypeStruct((t, D_MODEL), BF16)],
        scratch_shapes=[pltpu.VMEM((ts + HALO, D_MODEL), F32),
                        pltpu.VMEM((ts + HALO, D_MODEL), F32),
                        pltpu.VMEM((HALO, D_MODEL), F32)],
        compiler_params=_params(("arbitrary", "arbitrary"), VMEM_LIMIT),
        name="conv_rec",
    )(z, z, z, z, z, cw, rcw, rcb, wr, wi, br, bi, lam)


def _merge_kernel(a_ref, c_ref, r_ref, g0_ref, g1_ref, g2_ref, h_ref, wb_ref, wo_ref, o_ref):
    merged = None
    for n, (br, gt) in enumerate(((a_ref, g0_ref), (c_ref, g1_ref), (r_ref, g2_ref))):
        y = jnp.dot(br[...], wb_ref[n], preferred_element_type=F32)
        term = _sigmoid(gt[...].astype(F32)) * y
        merged = term if merged is None else merged + term
    o_ref[...] = h_ref[...] + jnp.dot(merged.astype(BF16), wo_ref[...],
                                      preferred_element_type=F32)


def merge(attn, conv, rec, z, h, wb, wo, tm=512):
    t = h.shape[0]
    tm = min(tm, t)
    blk = lambda col: pl.BlockSpec((tm, D_MODEL), lambda i: (i, col))
    return pl.pallas_call(
        _merge_kernel,
        grid=(t // tm,),
        in_specs=[blk(0), blk(0), blk(0), blk(COL_G0), blk(COL_G0 + 1), blk(COL_G0 + 2), blk(0),
                  pl.BlockSpec(wb.shape, lambda i: (0, 0, 0)),
                  pl.BlockSpec(wo.shape, lambda i: (0, 0))],
        out_specs=blk(0),
        out_shape=jax.ShapeDtypeStruct((t, D_MODEL), F32),
        compiler_params=_params(("arbitrary",), VMEM_LIMIT),
        name="merge",
    )(attn, conv, rec, z, z, z, h, wb, wo)


def _topk_rows(s, k):
    rows = s.shape[0]
    rid = lax.broadcasted_iota(jnp.int32, s.shape, 0)
    vals, idxs = [], []
    for _ in range(k):
        m = jnp.max(s, axis=0, keepdims=True)
        idx = jnp.min(jnp.where(s == m, rid, rows), axis=0, keepdims=True)
        vals.append(m)
        idxs.append(idx)
        s = jnp.where(rid == idx, -jnp.inf, s)
    return jnp.concatenate(vals, axis=0), jnp.concatenate(idxs, axis=0)


def _route_kernel(h_ref, g_ref, wq_ref, keys_ref, xn_ref, ids_ref, gates_ref):
    xn = _rms(h_ref[...], g_ref[...])
    xn_ref[...] = xn
    q = jnp.dot(xn.astype(BF16), wq_ref[...], preferred_element_type=F32)
    ids_parts, gate_parts = [], []
    for hd in range(PEER_HEADS):
        tops = []
        for p in range(2):
            c0 = (hd * 2 + p) * PEER_HALF
            sc = lax.dot_general(keys_ref[p], q[:, c0:c0 + PEER_HALF],
                                 (((1,), (1,)), ((), ())),
                                 precision=lax.Precision.HIGHEST,
                                 preferred_element_type=F32)
            tops.append(_topk_rows(sc, PEER_TOPK))
        (s1, i1), (s2, i2) = tops
        cand = jnp.concatenate([s1[a:a + 1, :] + s2 for a in range(PEER_TOPK)], axis=0)
        best_s, best_c = _topk_rows(cand, PEER_TOPK)
        a_idx = best_c >> 4
        b_idx = best_c & (PEER_TOPK - 1)
        e1 = jnp.zeros_like(best_c)
        e2 = jnp.zeros_like(best_c)
        for a in range(PEER_TOPK):
            e1 = jnp.where(a_idx == a, i1[a:a + 1, :], e1)
            e2 = jnp.where(b_idx == a, i2[a:a + 1, :], e2)
        expert = e1 * N_KEYS + e2
        m = jnp.max(best_s, axis=0, keepdims=True)
        ex = jnp.exp(best_s - m)
        gate_parts.append(ex / jnp.sum(ex, axis=0, keepdims=True))
        ids_parts.append(expert * SLAB)
    ids_ref[...] = jnp.concatenate(ids_parts, axis=0).T
    gates_ref[...] = jnp.concatenate(gate_parts, axis=0).T


def route(h, g, wq, keys, tm=128):
    t = h.shape[0]
    return pl.pallas_call(
        _route_kernel,
        grid=(t // tm,),
        in_specs=[pl.BlockSpec((tm, D_MODEL), lambda i: (i, 0)),
                  pl.BlockSpec((1, D_MODEL), lambda i: (0, 0)),
                  pl.BlockSpec(wq.shape, lambda i: (0, 0)),
                  pl.BlockSpec(keys.shape, lambda i: (0, 0, 0))],
        out_specs=[pl.BlockSpec((tm, D_MODEL), lambda i: (i, 0)),
                   pl.BlockSpec((tm, N_PAIRS), lambda i: (i, 0)),
                   pl.BlockSpec((tm, N_PAIRS), lambda i: (i, 0))],
        out_shape=[jax.ShapeDtypeStruct((t, D_MODEL), F32),
                   jax.ShapeDtypeStruct((t, N_PAIRS), jnp.int32),
                   jax.ShapeDtypeStruct((t, N_PAIRS), F32)],
        compiler_params=_params(("arbitrary",), VMEM_LIMIT),
        name="peer_route",
    )(h, g, wq, keys)


def _gather_tokens(ids_ref, tab_ref, tiles, t0):
    rows = [ids_ref.at[t0 + q] for q in range(len(tiles))]
    for j in range(N_PAIRS):
        for q, tile in enumerate(tiles):
            e = pl.multiple_of(rows[q][j], SLAB)
            tile[pl.ds(j, SLAB, stride=TILE_STRIDE), :] = tab_ref[pl.ds(e, SLAB), :]


def _tile_chunk(tile, s):
    return pltpu.bitcast(tile[pl.ds(s * TILE_STRIDE, N_PAIRS), :], BF16)


def _lane_parity(shape):
    lane = lax.broadcasted_iota(jnp.int32, shape, len(shape) - 1)
    return (lane & 1) == 1


def _peer_u_kernel(ids_ref, x_ref, gates_ref, tab_ref, w_ref, sc_ref, *tiles):
    tb = x_ref.shape[0]
    nfl = len(tiles)

    @pl.when(pl.program_id(0) == 0)
    def _():
        for tile in tiles:
            tile[...] = jnp.zeros_like(tile)

    def scores(t, tile):
        g = jnp.concatenate([_tile_chunk(tile, s) for s in range(SLAB)], axis=-1)
        xr = x_ref[pl.ds(t, 1), :]
        xa_hi, xa_lo = _split_bf16(xr[:, :ROW_WORDS])
        xb_hi, xb_lo = _split_bf16(xr[:, ROW_WORDS:])
        x8 = jnp.concatenate([xa_hi, xb_hi, xa_lo, xb_lo,
                              jnp.zeros((4, ROW_WORDS), BF16)], axis=0)
        res = lax.dot_general(x8, g, (((1,), (1,)), ((), ())),
                              preferred_element_type=F32)
        odd = _lane_parity((1, 2 * N_PAIRS))
        part = jnp.where(odd, res[1:2] + res[3:4], res[0:1] + res[2:3])
        sc_ref[pl.ds(t, 1), :] = part + pltpu.roll(part, 1, axis=1)

    def body(k, carry):
        t0 = nfl * k
        for q in range(nfl):
            scores(jnp.maximum(t0 - nfl + q, 0), tiles[q])
        _gather_tokens(ids_ref, tab_ref, tiles, t0)
        return carry

    lax.fori_loop(0, tb // nfl, body, 0)
    for q in range(nfl):
        scores(tb - nfl + q, tiles[q])

    r = lax.broadcasted_iota(jnp.int32, (N_PAIRS, 2 * N_PAIRS), 0)
    c = lax.broadcasted_iota(jnp.int32, (N_PAIRS, 2 * N_PAIRS), 1)
    spread = jnp.where(c == 2 * r + 1, 1.0, 0.0).astype(BF16)
    g_hi, g_lo = _split_bf16(gates_ref[...])
    g2 = (jnp.dot(g_hi, spread, preferred_element_type=F32)
          + jnp.dot(g_lo, spread, preferred_element_type=F32))
    w_ref[...] = g2 * _gelu(sc_ref[...])


def _table_spec():
    return pl.BlockSpec((N_EXPERTS * SLAB, LANES), lambda i: (0, 0),
                        pipeline_mode=pl.Buffered(1))


def _tile_scratch():
    return [pltpu.VMEM((SLAB * TILE_STRIDE, LANES), jnp.int32)
            for _ in range(TOKENS_IN_FLIGHT)]


def peer_scores(ids, xn, gates, tab, tb=64):
    t = xn.shape[0]
    return pl.pallas_call(
        _peer_u_kernel,
        grid=(t // tb,),
        in_specs=[pl.BlockSpec((tb, N_PAIRS), lambda i: (i, 0), memory_space=pltpu.SMEM),
                  pl.BlockSpec((tb, D_MODEL), lambda i: (i, 0)),
                  pl.BlockSpec((tb, N_PAIRS), lambda i: (i, 0)),
                  _table_spec()],
        out_specs=pl.BlockSpec((tb, 2 * N_PAIRS), lambda i: (i, 0)),
        out_shape=jax.ShapeDtypeStruct((t, 2 * N_PAIRS), F32),
        scratch_shapes=[pltpu.VMEM((tb, 2 * N_PAIRS), F32)] + _tile_scratch(),
        compiler_params=_params(("arbitrary",), VMEM_LIMIT),
        name="peer_u",
    )(ids, xn, gates, tab)


def _peer_v_kernel(ids_ref, w_ref, h_ref, tab_ref, o_ref, acc_ref, lhs_refs, *tiles):
    tb = h_ref.shape[0]
    nfl = len(tiles)

    @pl.when(pl.program_id(0) == 0)
    def _():
        for tile in tiles:
            tile[...] = jnp.zeros_like(tile)

    w_odd = w_ref[...]
    w_even = pltpu.roll(w_odd, 2 * N_PAIRS - 1, axis=1)
    for i, w in enumerate((w_even, w_odd)):
        hi = w.astype(BF16).astype(F32)
        lhs_refs[i, :, :] = hi
        lhs_refs[2 + i, :, :] = w - hi

    def combine(t, tile):
        rid = lax.broadcasted_iota(jnp.int32, (8, 2 * N_PAIRS), 0)
        lhs = jnp.zeros((8, 2 * N_PAIRS), F32)
        for i in range(4):
            row = lhs_refs[i, pl.ds(t, 1), :]
            lhs = jnp.where(rid == i, row, lhs)
        lhs = lhs.astype(BF16)
        lo, hi = [], []
        for s in range(SLAB):
            res = jnp.dot(lhs, _tile_chunk(tile, s), preferred_element_type=F32)
            lo.append(res[0:1] + res[2:3])
            hi.append(res[1:2] + res[3:4])
        acc_ref[pl.ds(t, 1), :] = jnp.concatenate(lo + hi, axis=-1)

    def body(k, carry):
        t0 = nfl * k
        for q in range(nfl):
            combine(jnp.maximum(t0 - nfl + q, 0), tiles[q])
        _gather_tokens(ids_ref, tab_ref, tiles, t0)
        return carry

    lax.fori_loop(0, tb // nfl, body, 0)
    for q in range(nfl):
        combine(tb - nfl + q, tiles[q])
    o_ref[...] = h_ref[...] + acc_ref[...]


def peer_combine(ids, w, h, tab, tb=64):
    t = h.shape[0]
    return pl.pallas_call(
        _peer_v_kernel,
        grid=(t // tb,),
        in_specs=[pl.BlockSpec((tb, N_PAIRS), lambda i: (i, 0), memory_space=pltpu.SMEM),
                  pl.BlockSpec((tb, 2 * N_PAIRS), lambda i: (i, 0)),
                  pl.BlockSpec((tb, D_MODEL), lambda i: (i, 0)),
                  _table_spec()],
        out_specs=pl.BlockSpec((tb, D_MODEL), lambda i: (i, 0)),
        out_shape=jax.ShapeDtypeStruct((t, D_MODEL), F32),
        scratch_shapes=[pltpu.VMEM((tb, D_MODEL), F32),
                        pltpu.VMEM((4, tb, 2 * N_PAIRS), F32)] + _tile_scratch(),
        compiler_params=_params(("arbitrary",), VMEM_LIMIT),
        name="peer_v",
    )(ids, w, h, tab)


def _ple_kernel(h_ref, p_ref, g_ref, wg_ref, wp_ref, gf_ref, o_ref, *, final):
    h = h_ref[...]
    gate = _sigmoid(jnp.dot(_rms(h, g_ref[...]).astype(BF16), wg_ref[...],
                            preferred_element_type=F32))
    out = h + gate * jnp.dot(p_ref[...].astype(BF16), wp_ref[...],
                             preferred_element_type=F32)
    if final:
        out = _rms(out, gf_ref[...])
    o_ref[...] = out


def ple(h, p, g, wg, wp, gf, final, tm=512):
    t = h.shape[0]
    tm = min(tm, t)
    return pl.pallas_call(
        functools.partial(_ple_kernel, final=final),
        grid=(t // tm,),
        in_specs=[pl.BlockSpec((tm, D_MODEL), lambda i: (i, 0)),
                  pl.BlockSpec((tm, PLE_DIM), lambda i: (i, 0)),
                  pl.BlockSpec((1, D_MODEL), lambda i: (0, 0)),
                  pl.BlockSpec(wg.shape, lambda i: (0, 0)),
                  pl.BlockSpec(wp.shape, lambda i: (0, 0)),
                  pl.BlockSpec((1, D_MODEL), lambda i: (0, 0))],
        out_specs=pl.BlockSpec((tm, D_MODEL), lambda i: (i, 0)),
        out_shape=jax.ShapeDtypeStruct((t, D_MODEL), F32),
        compiler_params=_params(("arbitrary",), VMEM_LIMIT),
        name="ple",
    )(h, p, g, wg, wp, gf)


def _permute_w_in(w):
    q, k, v, rest = (w[:, :D_MODEL], w[:, D_MODEL:D_MODEL + KV_W],
                     w[:, D_MODEL + KV_W:D_MODEL + 2 * KV_W], w[:, D_MODEL + 2 * KV_W:])
    return jnp.concatenate([q, rest, k, v], axis=1).astype(BF16)


def _pack_table(tab):
    t16 = lax.bitcast_convert_type(tab.astype(BF16), jnp.uint16).astype(jnp.uint32)
    words = t16[:, :ROW_WORDS] | (t16[:, ROW_WORDS:] << 16)
    return lax.bitcast_convert_type(words, jnp.int32).reshape(N_EXPERTS * SLAB, LANES)


def kernel(x, p, norm_mix, w_in, attn_sinks, conv_w, rec_conv_w, rec_conv_b, w_rgate, b_rgate, w_igate, b_igate, lru_lambda, w_branch, w_out, norm_ffn, w_peer_q, peer_sub_keys, peer_u, peer_v, norm_ple, w_ple_gate, w_ple_proj, norm_final):
    batch, seq, d = x.shape
    depth = w_in.shape[0]
    t = batch * seq
    row = lambda a: a.reshape(1, -1)
    h = x.reshape(t, d)
    for l in range(depth):
        z = norm_matmul(h, row(norm_mix[l]), _permute_w_in(w_in[l]))
        attn = attention(z, attn_sinks[l], batch, seq)
        conv, rec = conv_rec(z, conv_w[l], rec_conv_w[l], row(rec_conv_b[l]),
                             w_rgate[l].astype(BF16), w_igate[l].astype(BF16),
                             row(b_rgate[l]), row(b_igate[l]), row(lru_lambda[l]), batch, seq)
        h = merge(attn, conv, rec, z, h, w_branch[l].astype(BF16), w_out[l].astype(BF16))
        xn, ids, gates = route(h, row(norm_ffn[l]), w_peer_q[l].astype(BF16), peer_sub_keys[l])
        w = peer_scores(ids, xn, gates, _pack_table(peer_u[l]))
        h = peer_combine(ids, w, h, _pack_table(peer_v[l]))
        h = ple(h, p[l].reshape(t, -1), row(norm_ple[l]), w_ple_gate[l].astype(BF16),
                w_ple_proj[l].astype(BF16), row(norm_final), final=(l == depth - 1))
    return h.reshape(batch, seq, d)
```

```python
import functools

import jax
import jax.numpy as jnp
from jax import lax
from jax.experimental import pallas as pl
from jax.experimental.pallas import tpu as pltpu

F32 = jnp.float32
BF16 = jnp.bfloat16

D_MODEL = 1024
N_Q_HEADS = 16
N_KV_HEADS = 4
HEAD_DIM = 64
GROUP = N_Q_HEADS // N_KV_HEADS
ATTN_BLOCK = 128
KV_W = N_KV_HEADS * HEAD_DIM
REC_HEADS = 4
REC_HEAD_DIM = D_MODEL // REC_HEADS
LRU_C = 8.0
N_KEYS = 128
N_EXPERTS = N_KEYS * N_KEYS
PEER_HEADS = 8
PEER_TOPK = 16
PEER_HALF = 128
N_PAIRS = PEER_HEADS * PEER_TOPK
PLE_DIM = 256
EPS = 1e-6
NEG_INF = -1e30
TAG_NONE = float(1 << 20)
ROUTE_HEADS_PER_STEP = 4

COL_Q, COL_CB, COL_CC, COL_CX, COL_RX, COL_RY, COL_G0 = 0, 1, 2, 3, 4, 5, 6
IN_COLS = 9 * D_MODEL + 2 * KV_W
COL_K_KV = 9 * D_MODEL // KV_W
COL_V_KV = COL_K_KV + 1

LANES = 128
SLAB = D_MODEL // 2 // LANES
ROW_WORDS = SLAB * LANES
TILE_STRIDE = N_PAIRS + 8
TOKENS_IN_FLIGHT = 8
VMEM_LIMIT = 56 * 1024 * 1024


def _params(sem, vmem=None):
    return pltpu.CompilerParams(dimension_semantics=sem, vmem_limit_bytes=vmem)


def _rms(x32, g):
    ms = jnp.mean(x32 * x32, axis=-1, keepdims=True)
    return x32 * lax.rsqrt(ms + EPS) * g


def _sigmoid(x):
    return 1.0 / (1.0 + jnp.exp(-x))


def _gelu(x):
    return 0.5 * x * (1.0 + jnp.tanh(0.7978845608028654 * (x + 0.044715 * (x * x * x))))


def _split_bf16(x32):
    hi = x32.astype(BF16)
    lo = (x32 - hi.astype(F32)).astype(BF16)
    return hi, lo


def _norm_matmul_kernel(x_ref, g_ref, w_ref, o_ref, xn_ref):
    @pl.when(pl.program_id(1) == 0)
    def _():
        xn_ref[...] = _rms(x_ref[...], g_ref[...]).astype(BF16)

    o_ref[...] = jnp.dot(xn_ref[...], w_ref[...],
                         preferred_element_type=F32).astype(o_ref.dtype)


def norm_matmul(x, g, w, tm=2048, tn=512):
    m, d = x.shape
    n = w.shape[1]
    tm = min(tm, m)
    return pl.pallas_call(
        _norm_matmul_kernel,
        grid=(m // tm, n // tn),
        in_specs=[pl.BlockSpec((tm, d), lambda i, j: (i, 0)),
                  pl.BlockSpec((1, d), lambda i, j: (0, 0)),
                  pl.BlockSpec((d, tn), lambda i, j: (0, j))],
        out_specs=pl.BlockSpec((tm, tn), lambda i, j: (i, j)),
        out_shape=jax.ShapeDtypeStruct((m, n), BF16),
        scratch_shapes=[pltpu.VMEM((tm, d), BF16)],
        compiler_params=_params(("arbitrary", "arbitrary"), VMEM_LIMIT),
        name="norm_matmul",
    )(x, g, w)


def _attn_kernel(sink_ref, q_ref, kc_ref, vc_ref, kp_ref, vp_ref, o_ref):
    n = pl.program_id(1)
    q = q_ref[...]
    k = jnp.concatenate([kp_ref[...], kc_ref[...]], axis=0)
    v = jnp.concatenate([vp_ref[...], vc_ref[...]], axis=0)
    shape = (ATTN_BLOCK, 2 * ATTN_BLOCK)
    qi = lax.broadcasted_iota(jnp.int32, shape, 0)
    ki = lax.broadcasted_iota(jnp.int32, shape, 1)
    diff = qi + ATTN_BLOCK - ki
    first = jnp.where(n > 0, 0, ATTN_BLOCK)
    inside = jnp.where(diff >= 0, jnp.where(diff < ATTN_BLOCK, ki - first, -1), -1)
    mask = inside >= 0
    outs = []
    for h in range(N_KV_HEADS):
        kh = k[:, h * HEAD_DIM:(h + 1) * HEAD_DIM]
        vh = v[:, h * HEAD_DIM:(h + 1) * HEAD_DIM]
        for g in range(GROUP):
            hq = h * GROUP + g
            qh = q[:, hq * HEAD_DIM:(hq + 1) * HEAD_DIM]
            s = lax.dot_general(qh, kh, (((1,), (1,)), ((), ())),
                                preferred_element_type=F32) * (HEAD_DIM ** -0.5)
            s = jnp.where(mask, s, NEG_INF)
            sink = sink_ref[hq]
            m = jnp.maximum(jnp.max(s, axis=-1, keepdims=True), sink)
            e = jnp.exp(s - m)
            denom = jnp.sum(e, axis=-1, keepdims=True) + jnp.exp(sink - m)
            p = (e / denom).astype(BF16)
            outs.append(jnp.dot(p, vh, preferred_element_type=F32))
    o_ref[...] = jnp.concatenate(outs, axis=-1).astype(o_ref.dtype)


def attention(z, sinks, batch, seq):
    nb = seq // ATTN_BLOCK
    t = batch * seq
    cur = lambda b, n: b * nb + n
    prev = lambda b, n: b * nb + jnp.maximum(n - 1, 0)
    return pl.pallas_call(
        _attn_kernel,
        grid=(batch, nb),
        in_specs=[pl.BlockSpec(memory_space=pltpu.SMEM),
                  pl.BlockSpec((ATTN_BLOCK, D_MODEL), lambda b, n: (cur(b, n), COL_Q)),
                  pl.BlockSpec((ATTN_BLOCK, KV_W), lambda b, n: (cur(b, n), COL_K_KV)),
                  pl.BlockSpec((ATTN_BLOCK, KV_W), lambda b, n: (cur(b, n), COL_V_KV)),
                  pl.BlockSpec((ATTN_BLOCK, KV_W), lambda b, n: (prev(b, n), COL_K_KV)),
                  pl.BlockSpec((ATTN_BLOCK, KV_W), lambda b, n: (prev(b, n), COL_V_KV))],
        out_specs=pl.BlockSpec((ATTN_BLOCK, D_MODEL), lambda b, n: (cur(b, n), 0)),
        out_shape=jax.ShapeDtypeStruct((t, D_MODEL), BF16),
        compiler_params=_params(("arbitrary", "arbitrary")),
        name="swa_attention",
    )(sinks, z, z, z, z, z)


HALO = 8


def _convrec_kernel(cb_ref, cc_ref, cx_ref, rx_ref, ry_ref, cw_ref, rcw_ref, rcb_ref,
                    wr_ref, wi_ref, br_ref, bi_ref, lam_ref, conv_ref, rec_ref,
                    ubuf, xbuf, hcar):
    ts = cb_ref.shape[0]

    @pl.when(pl.program_id(1) == 0)
    def _():
        ubuf[0:HALO, :] = jnp.zeros((HALO, D_MODEL), F32)
        xbuf[0:HALO, :] = jnp.zeros((HALO, D_MODEL), F32)
        hcar[...] = jnp.zeros_like(hcar)

    u = cc_ref[...].astype(F32) * cx_ref[...].astype(F32)
    ubuf[HALO:HALO + ts, :] = u
    y = (cw_ref[2:3, :] * u + cw_ref[1:2, :] * ubuf[HALO - 1:HALO - 1 + ts, :]
         + cw_ref[0:1, :] * ubuf[HALO - 2:HALO - 2 + ts, :])
    conv_ref[...] = (cb_ref[...].astype(F32) * y).astype(conv_ref.dtype)
    ubuf[0:HALO, :] = ubuf[ts:ts + HALO, :]

    x = rx_ref[...].astype(F32)
    xbuf[HALO:HALO + ts, :] = x
    c = (rcw_ref[3:4, :] * x + rcw_ref[2:3, :] * xbuf[HALO - 1:HALO - 1 + ts, :]
         + rcw_ref[1:2, :] * xbuf[HALO - 2:HALO - 2 + ts, :]
         + rcw_ref[0:1, :] * xbuf[HALO - 3:HALO - 3 + ts, :] + rcb_ref[...])
    xbuf[0:HALO, :] = xbuf[ts:ts + HALO, :]
    cbf = c.astype(BF16)
    r_parts, i_parts = [], []
    for h in range(REC_HEADS):
        ch = cbf[:, h * REC_HEAD_DIM:(h + 1) * REC_HEAD_DIM]
        r_parts.append(jnp.dot(ch, wr_ref[h], preferred_element_type=F32))
        i_parts.append(jnp.dot(ch, wi_ref[h], preferred_element_type=F32))
    r = _sigmoid(jnp.concatenate(r_parts, axis=-1) + br_ref[...])
    ig = _sigmoid(jnp.concatenate(i_parts, axis=-1) + bi_ref[...])
    nl = -lam_ref[...]
    softplus = jnp.maximum(nl, 0.0) + jnp.log(1.0 + jnp.exp(-jnp.abs(nl)))
    log_a = -LRU_C * r * softplus
    a = jnp.exp(log_a)
    b = jnp.sqrt(1.0 - jnp.exp(2.0 * log_a)) * (ig * c)
    d = 1
    while d < ts:
        a_sh = jnp.concatenate([jnp.ones((d, D_MODEL), F32), a[:ts - d]], axis=0)
        b_sh = jnp.concatenate([jnp.zeros((d, D_MODEL), F32), b[:ts - d]], axis=0)
        b = a * b_sh + b
        a = a * a_sh
        d *= 2
    hs = b + a * hcar[0:1, :]
    hcar[0:1, :] = hs[ts - 1:ts, :]
    rec_ref[...] = (_gelu(ry_ref[...].astype(F32)) * hs).astype(rec_ref.dtype)


def conv_rec(z, cw, rcw, rcb, wr, wi, br, bi, lam, batch, seq, ts=256):
    ts = min(ts, seq)
    nt = seq // ts
    t = batch * seq
    row = lambda b, i: b * nt + i
    zspec = lambda col: pl.BlockSpec((ts, D_MODEL), lambda b, i: (row(b, i), col))
    full = lambda a: pl.BlockSpec(a.shape, lambda b, i: (0,) * a.ndim)
    return pl.pallas_call(
        _convrec_kernel,
        grid=(batch, nt),
        in_specs=[zspec(COL_CB), zspec(COL_CC), zspec(COL_CX), zspec(COL_RX), zspec(COL_RY),
                  full(cw), full(rcw), full(rcb), full(wr), full(wi), full(br), full(bi),
                  full(lam)],
        out_specs=[pl.BlockSpec((ts, D_MODEL), lambda b, i: (row(b, i), 0)),
                   pl.BlockSpec((ts, D_MODEL), lambda b, i: (row(b, i), 0))],
        out_shape=[jax.ShapeDtypeStruct((t, D_MODEL), BF16),
                   jax.ShapeDtypeStruct((t, D_MODEL), BF16)],
        scratch_shapes=[pltpu.VMEM((ts + HALO, D_MODEL), F32),
                        pltpu.VMEM((ts + HALO, D_MODEL), F32),
                        pltpu.VMEM((HALO, D_MODEL), F32)],
        compiler_params=_params(("arbitrary", "arbitrary"), VMEM_LIMIT),
        name="conv_rec",
    )(z, z, z, z, z, cw, rcw, rcb, wr, wi, br, bi, lam)


def _merge_kernel(a_ref, c_ref, r_ref, g0_ref, g1_ref, g2_ref, h_ref, wb_ref, wo_ref, o_ref):
    merged = None
    for n, (br, gt) in enumerate(((a_ref, g0_ref), (c_ref, g1_ref), (r_ref, g2_ref))):
        y = jnp.dot(br[...], wb_ref[n], preferred_element_type=F32)
        term = _sigmoid(gt[...].astype(F32)) * y
        merged = term if merged is None else merged + term
    o_ref[...] = h_ref[...] + jnp.dot(merged.astype(BF16), wo_ref[...],
                                      preferred_element_type=F32)


def merge(attn, conv, rec, z, h, wb, wo, tm=512):
    t = h.shape[0]
    tm = min(tm, t)
    blk = lambda col: pl.BlockSpec((tm, D_MODEL), lambda i: (i, col))
    return pl.pallas_call(
        _merge_kernel,
        grid=(t // tm,),
        in_specs=[blk(0), blk(0), blk(0), blk(COL_G0), blk(COL_G0 + 1), blk(COL_G0 + 2), blk(0),
                  pl.BlockSpec(wb.shape, lambda i: (0, 0, 0)),
                  pl.BlockSpec(wo.shape, lambda i: (0, 0))],
        out_specs=blk(0),
        out_shape=jax.ShapeDtypeStruct((t, D_MODEL), F32),
        compiler_params=_params(("arbitrary",), VMEM_LIMIT),
        name="merge",
    )(attn, conv, rec, z, z, z, h, wb, wo)


def _topk_rows(s, k, tag=None):
    if tag is None:
        tag = lax.broadcasted_iota(jnp.int32, s.shape, 0).astype(F32)
    vals, tags = [], []
    for _ in range(k):
        m = jnp.max(s, axis=0, keepdims=True)
        best = jnp.min(jnp.where(s == m, tag, TAG_NONE), axis=0, keepdims=True)
        vals.append(m)
        tags.append(best)
        s = jnp.where(tag == best, -jnp.inf, s)
    return jnp.concatenate(vals, axis=0), jnp.concatenate(tags, axis=0)


def _pair_candidates(s1, s2):
    k = PEER_TOPK
    tm = s1.shape[1]

    def seg(rows, vals, tag_of_row, lo, hi):
        r = lax.broadcasted_iota(jnp.int32, (rows, tm), 0)
        keep = jnp.where(r >= lo, r, hi + 1) <= hi
        return jnp.where(keep, vals, -jnp.inf), tag_of_row(r).astype(F32)

    segs = [seg(k, s1 + s2[0:1], lambda r: r * k, 0, k - 1),
            seg(k, s1[0:1] + s2, lambda r: r, 1, k - 1),
            seg(8, s1[0:8] + s2[1:2], lambda r: r * k + 1, 1, 7),
            seg(8, s1[1:2] + s2[0:8], lambda r: r + k, 2, 7),
            seg(8, s1[0:8] + s2[2:3], lambda r: r * k + 2, 2, 4),
            seg(8, s1[0:8] + s2[3:4], lambda r: r * k + 3, 2, 3),
            seg(8, s1[0:8] + s2[4:5], lambda r: r * k + 4, 2, 2)]
    return (jnp.concatenate([v for v, _ in segs], axis=0),
            jnp.concatenate([t for _, t in segs], axis=0))


def _route_kernel(h_ref, g_ref, wq_ref, keys_ref, xn_ref, ids_ref, gates_ref,
                  q_ref, ids_t, gates_t):
    xn = _rms(h_ref[...], g_ref[...])
    xn_ref[...] = xn
    q_ref[...] = jnp.dot(xn.astype(BF16), wq_ref[...], preferred_element_type=F32)

    def one_head(hd):
        tops = []
        for p in range(2):
            c0 = pl.multiple_of((hd * 2 + p) * PEER_HALF, PEER_HALF)
            sc = lax.dot_general(keys_ref[p], q_ref[:, pl.ds(c0, PEER_HALF)],
                                 (((1,), (1,)), ((), ())),
                                 precision=lax.Precision.HIGHEST,
                                 preferred_element_type=F32)
            tops.append(_topk_rows(sc, PEER_TOPK))
        (s1, i1), (s2, i2) = tops
        cand, flat = _pair_candidates(s1, s2)
        best_s, best_c = _topk_rows(cand, PEER_TOPK, tag=flat)
        flat_c = best_c.astype(jnp.int32)
        a_idx = flat_c >> 4
        b_idx = flat_c & (PEER_TOPK - 1)
        e1 = jnp.zeros_like(best_c)
        e2 = jnp.zeros_like(best_c)
        for a in range(PEER_TOPK):
            e1 = jnp.where(a_idx == a, i1[a:a + 1, :], e1)
            e2 = jnp.where(b_idx == a, i2[a:a + 1, :], e2)
        m = jnp.max(best_s, axis=0, keepdims=True)
        ex = jnp.exp(best_s - m)
        r0 = pl.multiple_of(hd * PEER_TOPK, PEER_TOPK)
        gates_t[pl.ds(r0, PEER_TOPK), :] = ex / jnp.sum(ex, axis=0, keepdims=True)
        ids_t[pl.ds(r0, PEER_TOPK), :] = ((e1 * N_KEYS + e2) * SLAB).astype(jnp.int32)

    def heads(step, carry):
        for i in range(ROUTE_HEADS_PER_STEP):
            one_head(step * ROUTE_HEADS_PER_STEP + i)
        return carry

    lax.fori_loop(0, PEER_HEADS // ROUTE_HEADS_PER_STEP, heads, 0)
    ids_ref[...] = ids_t[...].T
    gates_ref[...] = gates_t[...].T


def route(h, g, wq, keys, tm=128):
    t = h.shape[0]
    return pl.pallas_call(
        _route_kernel,
        grid=(t // tm,),
        in_specs=[pl.BlockSpec((tm, D_MODEL), lambda i: (i, 0)),
                  pl.BlockSpec((1, D_MODEL), lambda i: (0, 0)),
                  pl.BlockSpec(wq.shape, lambda i: (0, 0)),
                  pl.BlockSpec(keys.shape, lambda i: (0, 0, 0))],
        out_specs=[pl.BlockSpec((tm, D_MODEL), lambda i: (i, 0)),
                   pl.BlockSpec((tm, N_PAIRS), lambda i: (i, 0)),
                   pl.BlockSpec((tm, N_PAIRS), lambda i: (i, 0))],
        out_shape=[jax.ShapeDtypeStruct((t, D_MODEL), F32),
                   jax.ShapeDtypeStruct((t, N_PAIRS), jnp.int32),
                   jax.ShapeDtypeStruct((t, N_PAIRS), F32)],
        scratch_shapes=[pltpu.VMEM((tm, 2 * PEER_HALF * PEER_HEADS), F32),
                        pltpu.VMEM((N_PAIRS, tm), jnp.int32),
                        pltpu.VMEM((N_PAIRS, tm), F32)],
        compiler_params=_params(("arbitrary",), VMEM_LIMIT),
        name="peer_route",
    )(h, g, wq, keys)


def _gather_tokens(ids_ref, tab_ref, tiles, t0):
    rows = [ids_ref.at[t0 + q] for q in range(len(tiles))]
    for j in range(N_PAIRS):
        for q, tile in enumerate(tiles):
            e = pl.multiple_of(rows[q][j], SLAB)
            tile[pl.ds(j, SLAB, stride=TILE_STRIDE), :] = tab_ref[pl.ds(e, SLAB), :]


def _tile_chunk(tile, s):
    return pltpu.bitcast(tile[pl.ds(s * TILE_STRIDE, N_PAIRS), :], BF16)


def _lane_parity(shape):
    lane = lax.broadcasted_iota(jnp.int32, shape, len(shape) - 1)
    return (lane & 1) == 1


def _peer_u_kernel(ids_ref, x_ref, gates_ref, tab_ref, w_ref, sc_ref, *tiles):
    tb = x_ref.shape[0]
    nfl = len(tiles)

    @pl.when(pl.program_id(0) == 0)
    def _():
        for tile in tiles:
            tile[...] = jnp.zeros_like(tile)

    def scores(t, tile):
        g = jnp.concatenate([_tile_chunk(tile, s) for s in range(SLAB)], axis=-1)
        xr = x_ref[pl.ds(t, 1), :]
        xa_hi, xa_lo = _split_bf16(xr[:, :ROW_WORDS])
        xb_hi, xb_lo = _split_bf16(xr[:, ROW_WORDS:])
        x8 = jnp.concatenate([xa_hi, xb_hi, xa_lo, xb_lo,
                              jnp.zeros((4, ROW_WORDS), BF16)], axis=0)
        res = lax.dot_general(x8, g, (((1,), (1,)), ((), ())),
                              preferred_element_type=F32)
        odd = _lane_parity((1, 2 * N_PAIRS))
        part = jnp.where(odd, res[1:2] + res[3:4], res[0:1] + res[2:3])
        sc_ref[pl.ds(t, 1), :] = part + pltpu.roll(part, 1, axis=1)

    def body(k, carry):
        t0 = nfl * k
        for q in range(nfl):
            scores(jnp.maximum(t0 - nfl + q, 0), tiles[q])
        _gather_tokens(ids_ref, tab_ref, tiles, t0)
        return carry

    lax.fori_loop(0, tb // nfl, body, 0)
    for q in range(nfl):
        scores(tb - nfl + q, tiles[q])

    r = lax.broadcasted_iota(jnp.int32, (N_PAIRS, 2 * N_PAIRS), 0)
    c = lax.broadcasted_iota(jnp.int32, (N_PAIRS, 2 * N_PAIRS), 1)
    spread = jnp.where(c == 2 * r + 1, 1.0, 0.0).astype(BF16)
    g_hi, g_lo = _split_bf16(gates_ref[...])
    g2 = (jnp.dot(g_hi, spread, preferred_element_type=F32)
          + jnp.dot(g_lo, spread, preferred_element_type=F32))
    w_ref[...] = g2 * _gelu(sc_ref[...])


def _table_spec():
    return pl.BlockSpec((N_EXPERTS * SLAB, LANES), lambda i: (0, 0),
                        pipeline_mode=pl.Buffered(1))


def _tile_scratch():
    return [pltpu.VMEM((SLAB * TILE_STRIDE, LANES), jnp.int32)
            for _ in range(TOKENS_IN_FLIGHT)]


def peer_scores(ids, xn, gates, tab, tb=128):
    t = xn.shape[0]
    return pl.pallas_call(
        _peer_u_kernel,
        grid=(t // tb,),
        in_specs=[pl.BlockSpec((tb, N_PAIRS), lambda i: (i, 0), memory_space=pltpu.SMEM),
                  pl.BlockSpec((tb, D_MODEL), lambda i: (i, 0)),
                  pl.BlockSpec((tb, N_PAIRS), lambda i: (i, 0)),
                  _table_spec()],
        out_specs=pl.BlockSpec((tb, 2 * N_PAIRS), lambda i: (i, 0)),
        out_shape=jax.ShapeDtypeStruct((t, 2 * N_PAIRS), F32),
        scratch_shapes=[pltpu.VMEM((tb, 2 * N_PAIRS), F32)] + _tile_scratch(),
        compiler_params=_params(("arbitrary",), VMEM_LIMIT),
        name="peer_u",
    )(ids, xn, gates, tab)


def _peer_v_kernel(ids_ref, w_ref, h_ref, tab_ref, o_ref, acc_ref, lhs_refs, *tiles):
    tb = h_ref.shape[0]
    nfl = len(tiles)

    @pl.when(pl.program_id(0) == 0)
    def _():
        for tile in tiles:
            tile[...] = jnp.zeros_like(tile)

    w_odd = w_ref[...]
    w_even = pltpu.roll(w_odd, 2 * N_PAIRS - 1, axis=1)
    for i, w in enumerate((w_even, w_odd)):
        hi = w.astype(BF16).astype(F32)
        lhs_refs[i, :, :] = hi
        lhs_refs[2 + i, :, :] = w - hi

    def combine(t, tile):
        rid = lax.broadcasted_iota(jnp.int32, (8, 2 * N_PAIRS), 0)
        lhs = jnp.zeros((8, 2 * N_PAIRS), F32)
        for i in range(4):
            row = lhs_refs[i, pl.ds(t, 1), :]
            lhs = jnp.where(rid == i, row, lhs)
        lhs = lhs.astype(BF16)
        lo, hi = [], []
        for s in range(SLAB):
            res = jnp.dot(lhs, _tile_chunk(tile, s), preferred_element_type=F32)
            lo.append(res[0:1] + res[2:3])
            hi.append(res[1:2] + res[3:4])
        acc_ref[pl.ds(t, 1), :] = jnp.concatenate(lo + hi, axis=-1)

    def body(k, carry):
        t0 = nfl * k
        for q in range(nfl):
            combine(jnp.maximum(t0 - nfl + q, 0), tiles[q])
        _gather_tokens(ids_ref, tab_ref, tiles, t0)
        return carry

    lax.fori_loop(0, tb // nfl, body, 0)
    for q in range(nfl):
        combine(tb - nfl + q, tiles[q])
    o_ref[...] = h_ref[...] + acc_ref[...]


def peer_combine(ids, w, h, tab, tb=128):
    t = h.shape[0]
    return pl.pallas_call(
        _peer_v_kernel,
        grid=(t // tb,),
        in_specs=[pl.BlockSpec((tb, N_PAIRS), lambda i: (i, 0), memory_space=pltpu.SMEM),
                  pl.BlockSpec((tb, 2 * N_PAIRS), lambda i: (i, 0)),
                  pl.BlockSpec((tb, D_MODEL), lambda i: (i, 0)),
                  _table_spec()],
        out_specs=pl.BlockSpec((tb, D_MODEL), lambda i: (i, 0)),
        out_shape=jax.ShapeDtypeStruct((t, D_MODEL), F32),
        scratch_shapes=[pltpu.VMEM((tb, D_MODEL), F32),
                        pltpu.VMEM((4, tb, 2 * N_PAIRS), F32)] + _tile_scratch(),
        compiler_params=_params(("arbitrary",), VMEM_LIMIT),
        name="peer_v",
    )(ids, w, h, tab)


def _ple_kernel(h_ref, p_ref, g_ref, wg_ref, wp_ref, gf_ref, o_ref, *, final):
    h = h_ref[...]
    gate = _sigmoid(jnp.dot(_rms(h, g_ref[...]).astype(BF16), wg_ref[...],
                            preferred_element_type=F32))
    out = h + gate * jnp.dot(p_ref[...].astype(BF16), wp_ref[...],
                             preferred_element_type=F32)
    if final:
        out = _rms(out, gf_ref[...])
    o_ref[...] = out


def ple(h, p, g, wg, wp, gf, final, tm=512):
    t = h.shape[0]
    tm = min(tm, t)
    return pl.pallas_call(
        functools.partial(_ple_kernel, final=final),
        grid=(t // tm,),
        in_specs=[pl.BlockSpec((tm, D_MODEL), lambda i: (i, 0)),
                  pl.BlockSpec((tm, PLE_DIM), lambda i: (i, 0)),
                  pl.BlockSpec((1, D_MODEL), lambda i: (0, 0)),
                  pl.BlockSpec(wg.shape, lambda i: (0, 0)),
                  pl.BlockSpec(wp.shape, lambda i: (0, 0)),
                  pl.BlockSpec((1, D_MODEL), lambda i: (0, 0))],
        out_specs=pl.BlockSpec((tm, D_MODEL), lambda i: (i, 0)),
        out_shape=jax.ShapeDtypeStruct((t, D_MODEL), F32),
        compiler_params=_params(("arbitrary",), VMEM_LIMIT),
        name="ple",
    )(h, p, g, wg, wp, gf)


def _permute_w_in(w):
    q, k, v, rest = (w[:, :D_MODEL], w[:, D_MODEL:D_MODEL + KV_W],
                     w[:, D_MODEL + KV_W:D_MODEL + 2 * KV_W], w[:, D_MODEL + 2 * KV_W:])
    return jnp.concatenate([q, rest, k, v], axis=1).astype(BF16)


def _pack_table(tab):
    t16 = lax.bitcast_convert_type(tab.astype(BF16), jnp.uint16).astype(jnp.uint32)
    words = t16[:, :ROW_WORDS] | (t16[:, ROW_WORDS:] << 16)
    return lax.bitcast_convert_type(words, jnp.int32).reshape(N_EXPERTS * SLAB, LANES)


def kernel(x, p, norm_mix, w_in, attn_sinks, conv_w, rec_conv_w, rec_conv_b, w_rgate, b_rgate, w_igate, b_igate, lru_lambda, w_branch, w_out, norm_ffn, w_peer_q, peer_sub_keys, peer_u, peer_v, norm_ple, w_ple_gate, w_ple_proj, norm_final):
    batch, seq, d = x.shape
    depth = w_in.shape[0]
    t = batch * seq
    row = lambda a: a.reshape(1, -1)
    h = x.reshape(t, d)
    for l in range(depth):
        z = norm_matmul(h, row(norm_mix[l]), _permute_w_in(w_in[l]))
        attn = attention(z, attn_sinks[l], batch, seq)
        conv, rec = conv_rec(z, conv_w[l], rec_conv_w[l], row(rec_conv_b[l]),
                             w_rgate[l].astype(BF16), w_igate[l].astype(BF16),
                             row(b_rgate[l]), row(b_igate[l]), row(lru_lambda[l]), batch, seq)
        h = merge(attn, conv, rec, z, h, w_branch[l].astype(BF16), w_out[l].astype(BF16))
        xn, ids, gates = route(h, row(norm_ffn[l]), w_peer_q[l].astype(BF16), peer_sub_keys[l])
        w = peer_scores(ids, xn, gates, _pack_table(peer_u[l]))
        h = peer_combine(ids, w, h, _pack_table(peer_v[l]))
        h = ple(h, p[l].reshape(t, -1), row(norm_ple[l]), w_ple_gate[l].astype(BF16),
                w_ple_proj[l].astype(BF16), row(norm_final), final=(l == depth - 1))
    return h.reshape(batch, seq, d)
```

```python
import functools

import jax
import jax.numpy as jnp
from jax import lax
from jax.experimental import pallas as pl
from jax.experimental.pallas import tpu as pltpu

F32 = jnp.float32
BF16 = jnp.bfloat16

D_MODEL = 1024
N_Q_HEADS = 16
N_KV_HEADS = 4
HEAD_DIM = 64
GROUP = N_Q_HEADS // N_KV_HEADS
ATTN_BLOCK = 128
KV_W = N_KV_HEADS * HEAD_DIM
REC_HEADS = 4
REC_HEAD_DIM = D_MODEL // REC_HEADS
LRU_C = 8.0
N_KEYS = 128
N_EXPERTS = N_KEYS * N_KEYS
PEER_HEADS = 8
PEER_TOPK = 16
PEER_HALF = 128
N_PAIRS = PEER_HEADS * PEER_TOPK
PLE_DIM = 256
EPS = 1e-6
NEG_INF = -1e30
TAG_NONE = float(1 << 20)
ROUTE_HEADS_PER_STEP = 8

COL_Q, COL_CB, COL_CC, COL_CX, COL_RX, COL_RY, COL_G0 = 0, 1, 2, 3, 4, 5, 6
IN_COLS = 9 * D_MODEL + 2 * KV_W
COL_K_KV = 9 * D_MODEL // KV_W
COL_V_KV = COL_K_KV + 1

LANES = 128
SLAB = D_MODEL // 2 // LANES
ROW_WORDS = SLAB * LANES
TILE_STRIDE = N_PAIRS + 8
SCORES_TOKENS_IN_FLIGHT = 16
COMBINE_TOKENS_IN_FLIGHT = 16
VMEM_LIMIT = 56 * 1024 * 1024


def _params(sem, vmem=None):
    return pltpu.CompilerParams(dimension_semantics=sem, vmem_limit_bytes=vmem)


def _rms(x32, g):
    ms = jnp.mean(x32 * x32, axis=-1, keepdims=True)
    return x32 * lax.rsqrt(ms + EPS) * g


def _sigmoid(x):
    return 1.0 / (1.0 + jnp.exp(-x))


def _gelu(x):
    return 0.5 * x * (1.0 + jnp.tanh(0.7978845608028654 * (x + 0.044715 * (x * x * x))))


def _split_bf16(x32):
    hi = x32.astype(BF16)
    lo = (x32 - hi.astype(F32)).astype(BF16)
    return hi, lo


def _norm_matmul_kernel(x_ref, g_ref, w_ref, o_ref, xn_ref):
    @pl.when(pl.program_id(1) == 0)
    def _():
        xn_ref[...] = _rms(x_ref[...], g_ref[...]).astype(BF16)

    o_ref[...] = jnp.dot(xn_ref[...], w_ref[...],
                         preferred_element_type=F32).astype(o_ref.dtype)


def norm_matmul(x, g, w, tm=2048, tn=512):
    m, d = x.shape
    n = w.shape[1]
    tm = min(tm, m)
    return pl.pallas_call(
        _norm_matmul_kernel,
        grid=(m // tm, n // tn),
        in_specs=[pl.BlockSpec((tm, d), lambda i, j: (i, 0)),
                  pl.BlockSpec((1, d), lambda i, j: (0, 0)),
                  pl.BlockSpec((d, tn), lambda i, j: (0, j))],
        out_specs=pl.BlockSpec((tm, tn), lambda i, j: (i, j)),
        out_shape=jax.ShapeDtypeStruct((m, n), BF16),
        scratch_shapes=[pltpu.VMEM((tm, d), BF16)],
        compiler_params=_params(("arbitrary", "arbitrary"), VMEM_LIMIT),
        name="norm_matmul",
    )(x, g, w)


def _attn_kernel(sink_ref, q_ref, kc_ref, vc_ref, kp_ref, vp_ref, o_ref):
    n = pl.program_id(1)
    q = q_ref[...]
    k = jnp.concatenate([kp_ref[...], kc_ref[...]], axis=0)
    v = jnp.concatenate([vp_ref[...], vc_ref[...]], axis=0)
    shape = (ATTN_BLOCK, 2 * ATTN_BLOCK)
    qi = lax.broadcasted_iota(jnp.int32, shape, 0)
    ki = lax.broadcasted_iota(jnp.int32, shape, 1)
    diff = qi + ATTN_BLOCK - ki
    first = jnp.where(n > 0, 0, ATTN_BLOCK)
    inside = jnp.where(diff >= 0, jnp.where(diff < ATTN_BLOCK, ki - first, -1), -1)
    mask = inside >= 0
    outs = []
    for h in range(N_KV_HEADS):
        kh = k[:, h * HEAD_DIM:(h + 1) * HEAD_DIM]
        vh = v[:, h * HEAD_DIM:(h + 1) * HEAD_DIM]
        for g in range(GROUP):
            hq = h * GROUP + g
            qh = q[:, hq * HEAD_DIM:(hq + 1) * HEAD_DIM]
            s = lax.dot_general(qh, kh, (((1,), (1,)), ((), ())),
                                preferred_element_type=F32) * (HEAD_DIM ** -0.5)
            s = jnp.where(mask, s, NEG_INF)
            sink = sink_ref[hq]
            m = jnp.maximum(jnp.max(s, axis=-1, keepdims=True), sink)
            e = jnp.exp(s - m)
            denom = jnp.sum(e, axis=-1, keepdims=True) + jnp.exp(sink - m)
            p = (e / denom).astype(BF16)
            outs.append(jnp.dot(p, vh, preferred_element_type=F32))
    o_ref[...] = jnp.concatenate(outs, axis=-1).astype(o_ref.dtype)


def attention(z, sinks, batch, seq):
    nb = seq // ATTN_BLOCK
    t = batch * seq
    cur = lambda b, n: b * nb + n
    prev = lambda b, n: b * nb + jnp.maximum(n - 1, 0)
    return pl.pallas_call(
        _attn_kernel,
        grid=(batch, nb),
        in_specs=[pl.BlockSpec(memory_space=pltpu.SMEM),
                  pl.BlockSpec((ATTN_BLOCK, D_MODEL), lambda b, n: (cur(b, n), COL_Q)),
                  pl.BlockSpec((ATTN_BLOCK, KV_W), lambda b, n: (cur(b, n), COL_K_KV)),
                  pl.BlockSpec((ATTN_BLOCK, KV_W), lambda b, n: (cur(b, n), COL_V_KV)),
                  pl.BlockSpec((ATTN_BLOCK, KV_W), lambda b, n: (prev(b, n), COL_K_KV)),
                  pl.BlockSpec((ATTN_BLOCK, KV_W), lambda b, n: (prev(b, n), COL_V_KV))],
        out_specs=pl.BlockSpec((ATTN_BLOCK, D_MODEL), lambda b, n: (cur(b, n), 0)),
        out_shape=jax.ShapeDtypeStruct((t, D_MODEL), BF16),
        compiler_params=_params(("arbitrary", "arbitrary")),
        name="swa_attention",
    )(sinks, z, z, z, z, z)


HALO = 8


def _convrec_kernel(cb_ref, cc_ref, cx_ref, rx_ref, ry_ref, cw_ref, rcw_ref, rcb_ref,
                    wr_ref, wi_ref, br_ref, bi_ref, lam_ref, conv_ref, rec_ref,
                    ubuf, xbuf, hcar):
    ts = cb_ref.shape[0]

    @pl.when(pl.program_id(1) == 0)
    def _():
        ubuf[0:HALO, :] = jnp.zeros((HALO, D_MODEL), F32)
        xbuf[0:HALO, :] = jnp.zeros((HALO, D_MODEL), F32)
        hcar[...] = jnp.zeros_like(hcar)

    u = cc_ref[...].astype(F32) * cx_ref[...].astype(F32)
    ubuf[HALO:HALO + ts, :] = u
    y = (cw_ref[2:3, :] * u + cw_ref[1:2, :] * ubuf[HALO - 1:HALO - 1 + ts, :]
         + cw_ref[0:1, :] * ubuf[HALO - 2:HALO - 2 + ts, :])
    conv_ref[...] = (cb_ref[...].astype(F32) * y).astype(conv_ref.dtype)
    ubuf[0:HALO, :] = ubuf[ts:ts + HALO, :]

    x = rx_ref[...].astype(F32)
    xbuf[HALO:HALO + ts, :] = x
    c = (rcw_ref[3:4, :] * x + rcw_ref[2:3, :] * xbuf[HALO - 1:HALO - 1 + ts, :]
         + rcw_ref[1:2, :] * xbuf[HALO - 2:HALO - 2 + ts, :]
         + rcw_ref[0:1, :] * xbuf[HALO - 3:HALO - 3 + ts, :] + rcb_ref[...])
    xbuf[0:HALO, :] = xbuf[ts:ts + HALO, :]
    cbf = c.astype(BF16)
    r_parts, i_parts = [], []
    for h in range(REC_HEADS):
        ch = cbf[:, h * REC_HEAD_DIM:(h + 1) * REC_HEAD_DIM]
        r_parts.append(jnp.dot(ch, wr_ref[h], preferred_element_type=F32))
        i_parts.append(jnp.dot(ch, wi_ref[h], preferred_element_type=F32))
    r = _sigmoid(jnp.concatenate(r_parts, axis=-1) + br_ref[...])
    ig = _sigmoid(jnp.concatenate(i_parts, axis=-1) + bi_ref[...])
    nl = -lam_ref[...]
    softplus = jnp.maximum(nl, 0.0) + jnp.log(1.0 + jnp.exp(-jnp.abs(nl)))
    log_a = -LRU_C * r * softplus
    a = jnp.exp(log_a)
    b = jnp.sqrt(1.0 - jnp.exp(2.0 * log_a)) * (ig * c)
    d = 1
    while d < ts:
        a_sh = jnp.concatenate([jnp.ones((d, D_MODEL), F32), a[:ts - d]], axis=0)
        b_sh = jnp.concatenate([jnp.zeros((d, D_MODEL), F32), b[:ts - d]], axis=0)
        b = a * b_sh + b
        a = a * a_sh
        d *= 2
    hs = b + a * hcar[0:1, :]
    hcar[0:1, :] = hs[ts - 1:ts, :]
    rec_ref[...] = (_gelu(ry_ref[...].astype(F32)) * hs).astype(rec_ref.dtype)


def conv_rec(z, cw, rcw, rcb, wr, wi, br, bi, lam, batch, seq, ts=256):
    ts = min(ts, seq)
    nt = seq // ts
    t = batch * seq
    row = lambda b, i: b * nt + i
    zspec = lambda col: pl.BlockSpec((ts, D_MODEL), lambda b, i: (row(b, i), col))
    full = lambda a: pl.BlockSpec(a.shape, lambda b, i: (0,) * a.ndim)
    return pl.pallas_call(
        _convrec_kernel,
        grid=(batch, nt),
        in_specs=[zspec(COL_CB), zspec(COL_CC), zspec(COL_CX), zspec(COL_RX), zspec(COL_RY),
                  full(cw), full(rcw), full(rcb), full(wr), full(wi), full(br), full(bi),
                  full(lam)],
        out_specs=[pl.BlockSpec((ts, D_MODEL), lambda b, i: (row(b, i), 0)),
                   pl.BlockSpec((ts, D_MODEL), lambda b, i: (row(b, i), 0))],
        out_shape=[jax.ShapeDtypeStruct((t, D_MODEL), BF16),
                   jax.ShapeDtypeStruct((t, D_MODEL), BF16)],
        scratch_shapes=[pltpu.VMEM((ts + HALO, D_MODEL), F32),
                        pltpu.VMEM((ts + HALO, D_MODEL), F32),
                        pltpu.VMEM((HALO, D_MODEL), F32)],
        compiler_params=_params(("arbitrary", "arbitrary"), VMEM_LIMIT),
        name="conv_rec",
    )(z, z, z, z, z, cw, rcw, rcb, wr, wi, br, bi, lam)


def _merge_kernel(a_ref, c_ref, r_ref, g0_ref, g1_ref, g2_ref, h_ref, wb_ref, wo_ref, o_ref):
    merged = None
    for n, (br, gt) in enumerate(((a_ref, g0_ref), (c_ref, g1_ref), (r_ref, g2_ref))):
        y = jnp.dot(br[...], wb_ref[n], preferred_element_type=F32)
        term = _sigmoid(gt[...].astype(F32)) * y
        merged = term if merged is None else merged + term
    o_ref[...] = h_ref[...] + jnp.dot(merged.astype(BF16), wo_ref[...],
                                      preferred_element_type=F32)


def merge(attn, conv, rec, z, h, wb, wo, tm=512):
    t = h.shape[0]
    tm = min(tm, t)
    blk = lambda col: pl.BlockSpec((tm, D_MODEL), lambda i: (i, col))
    return pl.pallas_call(
        _merge_kernel,
        grid=(t // tm,),
        in_specs=[blk(0), blk(0), blk(0), blk(COL_G0), blk(COL_G0 + 1), blk(COL_G0 + 2), blk(0),
                  pl.BlockSpec(wb.shape, lambda i: (0, 0, 0)),
                  pl.BlockSpec(wo.shape, lambda i: (0, 0))],
        out_specs=blk(0),
        out_shape=jax.ShapeDtypeStruct((t, D_MODEL), F32),
        compiler_params=_params(("arbitrary",), VMEM_LIMIT),
        name="merge",
    )(attn, conv, rec, z, z, z, h, wb, wo)


def _sorting_network(n):
    pairs = []
    p = 1
    while p < n:
        k = p
        while k >= 1:
            for j in range(k % p, n - k, 2 * k):
                for i in range(min(k, n - j - k)):
                    if (i + j) // (2 * p) == (i + j + k) // (2 * p):
                        pairs.append((i + j, i + j + k))
            k //= 2
        p *= 2
    return pairs


def _topk_columns(vals, tags, k):
    vals, tags = list(vals), list(tags)
    n = len(vals)
    for i, j in _sorting_network(n):
        va, ta, vb, tb = vals[i], tags[i], vals[j], tags[j]
        first = (va > vb) | ((va == vb) & (ta < tb))
        vals[i], vals[j] = jnp.where(first, va, vb), jnp.where(first, vb, va)
        tags[i], tags[j] = jnp.where(first, ta, tb), jnp.where(first, tb, ta)
    out_v, out_t = [], []
    for it in range(k):
        m = jnp.max(vals[0], axis=0, keepdims=True)
        best = jnp.min(jnp.where(vals[0] == m, tags[0], TAG_NONE), axis=0, keepdims=True)
        out_v.append(m)
        out_t.append(best)
        left = k - 1 - it
        pop = tags[0] == best
        for d in range(min(left, n - 1)):
            vals[d] = jnp.where(pop, vals[d + 1], vals[d])
            tags[d] = jnp.where(pop, tags[d + 1], tags[d])
        if left > n - 1:
            vals[n - 1] = jnp.where(pop, -jnp.inf, vals[n - 1])
            tags[n - 1] = jnp.where(pop, TAG_NONE, tags[n - 1])
    return out_v, out_t


def _stack_rows(rows, lo, hi):
    n = hi - lo
    rid = lax.broadcasted_iota(jnp.int32, (n, rows[0].shape[1]), 0)
    out = jnp.broadcast_to(rows[lo], (n, rows[0].shape[1]))
    for i in range(1, n):
        out = jnp.where(rid == i, rows[lo + i], out)
    return out


def _pair_candidates(s1, s2):
    k = PEER_TOPK
    tm = s1[0].shape[1]
    r = lax.broadcasted_iota(jnp.int32, (8, tm), 0)
    a_lo, a_hi = _stack_rows(s1, 0, 8), _stack_rows(s1, 8, 16)
    b_lo, b_hi = _stack_rows(s2, 0, 8), _stack_rows(s2, 8, 16)

    def slab(vals, tag, lo, hi):
        keep = jnp.where(r >= lo, r, hi + 1) <= hi
        return jnp.where(keep, vals, -jnp.inf), tag.astype(F32)

    slabs = [slab(a_lo + s2[0], r * k, 0, 7),
             slab(a_hi + s2[0], (r + 8) * k, 0, 7),
             slab(s1[0] + b_lo, r, 1, 7),
             slab(s1[0] + b_hi, r + 8, 0, 7),
             slab(a_lo + s2[1], r * k + 1, 1, 7),
             slab(s1[1] + b_lo, r + k, 2, 7),
             slab(a_lo + s2[2], r * k + 2, 2, 4),
             slab(a_lo + s2[3], r * k + 3, 2, 3),
             slab(a_lo + s2[4], r * k + 4, 2, 2)]
    return [v for v, _ in slabs], [t for _, t in slabs]


def _route_kernel(h_ref, g_ref, wq_ref, keys_ref, xn_ref, ids_ref, gates_ref,
                  q_ref, ids_t, gates_t):
    xn = _rms(h_ref[...], g_ref[...])
    xn_ref[...] = xn
    q_ref[...] = jnp.dot(xn.astype(BF16), wq_ref[...], preferred_element_type=F32)
    tm = h_ref.shape[0]
    row8 = lax.broadcasted_iota(jnp.int32, (8, tm), 0)
    key_tags = [(row8 + 8 * g).astype(F32) for g in range(N_KEYS // 8)]

    def one_head(hd):
        tops = []
        for p in range(2):
            c0 = pl.multiple_of((hd * 2 + p) * PEER_HALF, PEER_HALF)
            sc = lax.dot_general(keys_ref[p], q_ref[:, pl.ds(c0, PEER_HALF)],
                                 (((1,), (1,)), ((), ())),
                                 precision=lax.Precision.HIGHEST,
                                 preferred_element_type=F32)
            slabs = [sc[8 * g:8 * g + 8, :] for g in range(N_KEYS // 8)]
            tops.append(_topk_columns(slabs, key_tags, PEER_TOPK))
        (s1, i1), (s2, i2) = tops
        cand, flat = _pair_candidates(s1, s2)
        best_s, best_c = _topk_columns(cand, flat, PEER_TOPK)
        best_s = _stack_rows(best_s, 0, PEER_TOPK)
        flat_c = _stack_rows(best_c, 0, PEER_TOPK).astype(jnp.int32)
        a_idx = flat_c >> 4
        b_idx = flat_c & (PEER_TOPK - 1)
        e1 = jnp.zeros(flat_c.shape, F32)
        e2 = jnp.zeros(flat_c.shape, F32)
        for a in range(PEER_TOPK):
            e1 = jnp.where(a_idx == a, i1[a], e1)
            e2 = jnp.where(b_idx == a, i2[a], e2)
        ex = jnp.exp(best_s - best_s[0:1, :])
        r0 = pl.multiple_of(hd * PEER_TOPK, PEER_TOPK)
        gates_t[pl.ds(r0, PEER_TOPK), :] = ex / jnp.sum(ex, axis=0, keepdims=True)
        ids_t[pl.ds(r0, PEER_TOPK), :] = ((e1 * N_KEYS + e2) * SLAB).astype(jnp.int32)

    def heads(step, carry):
        for i in range(ROUTE_HEADS_PER_STEP):
            one_head(step * ROUTE_HEADS_PER_STEP + i)
        return carry

    lax.fori_loop(0, PEER_HEADS // ROUTE_HEADS_PER_STEP, heads, 0)
    ids_ref[...] = ids_t[...].T
    gates_ref[...] = gates_t[...].T


def route(h, g, wq, keys, tm=128):
    t = h.shape[0]
    return pl.pallas_call(
        _route_kernel,
        grid=(t // tm,),
        in_specs=[pl.BlockSpec((tm, D_MODEL), lambda i: (i, 0)),
                  pl.BlockSpec((1, D_MODEL), lambda i: (0, 0)),
                  pl.BlockSpec(wq.shape, lambda i: (0, 0)),
                  pl.BlockSpec(keys.shape, lambda i: (0, 0, 0))],
        out_specs=[pl.BlockSpec((tm, D_MODEL), lambda i: (i, 0)),
                   pl.BlockSpec((tm, N_PAIRS), lambda i: (i, 0)),
                   pl.BlockSpec((tm, N_PAIRS), lambda i: (i, 0))],
        out_shape=[jax.ShapeDtypeStruct((t, D_MODEL), F32),
                   jax.ShapeDtypeStruct((t, N_PAIRS), jnp.int32),
                   jax.ShapeDtypeStruct((t, N_PAIRS), F32)],
        scratch_shapes=[pltpu.VMEM((tm, 2 * PEER_HALF * PEER_HEADS), F32),
                        pltpu.VMEM((N_PAIRS, tm), jnp.int32),
                        pltpu.VMEM((N_PAIRS, tm), F32)],
        compiler_params=_params(("arbitrary",), VMEM_LIMIT),
        name="peer_route",
    )(h, g, wq, keys)


def _gather_tokens(ids_ref, tab_ref, tiles, t0):
    zero = lax.min(t0, 0)
    offs = [zero + u for u in range(8)]
    for jj in range(N_PAIRS // 8):
        rows = [ids_ref.at[t0 + q, pl.ds(8 * jj, 8)] for q in range(len(tiles))]
        for u in range(8):
            j = 8 * jj + u
            for q, tile in enumerate(tiles):
                e = pl.multiple_of(rows[q][offs[u]], SLAB)
                tile[pl.ds(j, SLAB, stride=TILE_STRIDE), :] = tab_ref[pl.ds(e, SLAB), :]


def _tile_chunk(tile, s):
    return pltpu.bitcast(tile[pl.ds(s * TILE_STRIDE, N_PAIRS), :], BF16)


def _lane_parity(shape):
    lane = lax.broadcasted_iota(jnp.int32, shape, len(shape) - 1)
    return (lane & 1) == 1


def _peer_u_kernel(ids_ref, x_ref, gates_ref, tab_ref, w_ref, sc_ref, *tiles):
    tb = x_ref.shape[0]
    nfl = len(tiles)

    @pl.when(pl.program_id(0) == 0)
    def _():
        for tile in tiles:
            tile[...] = jnp.zeros_like(tile)

    def scores(t, tile):
        g = jnp.concatenate([_tile_chunk(tile, s) for s in range(SLAB)], axis=-1)
        xr = x_ref[pl.ds(t, 1), :]
        xa_hi, xa_lo = _split_bf16(xr[:, :ROW_WORDS])
        xb_hi, xb_lo = _split_bf16(xr[:, ROW_WORDS:])
        x8 = jnp.concatenate([xa_hi, xb_hi, xa_lo, xb_lo,
                              jnp.zeros((4, ROW_WORDS), BF16)], axis=0)
        res = lax.dot_general(x8, g, (((1,), (1,)), ((), ())),
                              preferred_element_type=F32)
        odd = _lane_parity((1, 2 * N_PAIRS))
        part = jnp.where(odd, res[1:2] + res[3:4], res[0:1] + res[2:3])
        sc_ref[pl.ds(t, 1), :] = part + pltpu.roll(part, 1, axis=1)

    def body(k, carry):
        t0 = nfl * k
        for q in range(nfl):
            scores(jnp.maximum(t0 - nfl + q, 0), tiles[q])
        _gather_tokens(ids_ref, tab_ref, tiles, t0)
        return carry

    lax.fori_loop(0, tb // nfl, body, 0)
    for q in range(nfl):
        scores(tb - nfl + q, tiles[q])

    r = lax.broadcasted_iota(jnp.int32, (N_PAIRS, 2 * N_PAIRS), 0)
    c = lax.broadcasted_iota(jnp.int32, (N_PAIRS, 2 * N_PAIRS), 1)
    spread = jnp.where(c == 2 * r + 1, 1.0, 0.0).astype(BF16)
    g_hi, g_lo = _split_bf16(gates_ref[...])
    g2 = (jnp.dot(g_hi, spread, preferred_element_type=F32)
          + jnp.dot(g_lo, spread, preferred_element_type=F32))
    w_ref[...] = g2 * _gelu(sc_ref[...])


def _table_spec():
    return pl.BlockSpec((N_EXPERTS * SLAB, LANES), lambda i: (0, 0),
                        pipeline_mode=pl.Buffered(1))


def _tile_scratch(n):
    return [pltpu.VMEM((SLAB * TILE_STRIDE, LANES), jnp.int32) for _ in range(n)]


def peer_scores(ids, xn, gates, tab, tb=512):
    t = xn.shape[0]
    return pl.pallas_call(
        _peer_u_kernel,
        grid=(t // tb,),
        in_specs=[pl.BlockSpec((tb, N_PAIRS), lambda i: (i, 0), memory_space=pltpu.SMEM),
                  pl.BlockSpec((tb, D_MODEL), lambda i: (i, 0)),
                  pl.BlockSpec((tb, N_PAIRS), lambda i: (i, 0)),
                  _table_spec()],
        out_specs=pl.BlockSpec((tb, 2 * N_PAIRS), lambda i: (i, 0)),
        out_shape=jax.ShapeDtypeStruct((t, 2 * N_PAIRS), F32),
        scratch_shapes=([pltpu.VMEM((tb, 2 * N_PAIRS), F32)]
                        + _tile_scratch(SCORES_TOKENS_IN_FLIGHT)),
        compiler_params=_params(("arbitrary",), VMEM_LIMIT),
        name="peer_u",
    )(ids, xn, gates, tab)


def _peer_v_kernel(ids_ref, w_ref, h_ref, tab_ref, o_ref, acc_ref, lhs_refs, *tiles):
    tb = h_ref.shape[0]
    nfl = len(tiles)

    @pl.when(pl.program_id(0) == 0)
    def _():
        for tile in tiles:
            tile[...] = jnp.zeros_like(tile)

    w_odd = w_ref[...]
    w_even = pltpu.roll(w_odd, 2 * N_PAIRS - 1, axis=1)
    for i, w in enumerate((w_even, w_odd)):
        hi = w.astype(BF16).astype(F32)
        lhs_refs[i, :, :] = hi
        lhs_refs[2 + i, :, :] = w - hi

    def combine(t, tile):
        rid = lax.broadcasted_iota(jnp.int32, (8, 2 * N_PAIRS), 0)
        lhs = jnp.zeros((8, 2 * N_PAIRS), F32)
        for i in range(4):
            row = lhs_refs[i, pl.ds(t, 1), :]
            lhs = jnp.where(rid == i, row, lhs)
        lhs = lhs.astype(BF16)
        lo, hi = [], []
        for s in range(SLAB):
            res = jnp.dot(lhs, _tile_chunk(tile, s), preferred_element_type=F32)
            lo.append(res[0:1] + res[2:3])
            hi.append(res[1:2] + res[3:4])
        acc_ref[pl.ds(t, 1), :] = jnp.concatenate(lo + hi, axis=-1)

    def body(k, carry):
        t0 = nfl * k
        for q in range(nfl):
            combine(jnp.maximum(t0 - nfl + q, 0), tiles[q])
        _gather_tokens(ids_ref, tab_ref, tiles, t0)
        return carry

    lax.fori_loop(0, tb // nfl, body, 0)
    for q in range(nfl):
        combine(tb - nfl + q, tiles[q])
    o_ref[...] = h_ref[...] + acc_ref[...]


def peer_combine(ids, w, h, tab, tb=512):
    t = h.shape[0]
    return pl.pallas_call(
        _peer_v_kernel,
        grid=(t // tb,),
        in_specs=[pl.BlockSpec((tb, N_PAIRS), lambda i: (i, 0), memory_space=pltpu.SMEM),
                  pl.BlockSpec((tb, 2 * N_PAIRS), lambda i: (i, 0)),
                  pl.BlockSpec((tb, D_MODEL), lambda i: (i, 0)),
                  _table_spec()],
        out_specs=pl.BlockSpec((tb, D_MODEL), lambda i: (i, 0)),
        out_shape=jax.ShapeDtypeStruct((t, D_MODEL), F32),
        scratch_shapes=[pltpu.VMEM((tb, D_MODEL), F32),
                        pltpu.VMEM((4, tb, 2 * N_PAIRS), F32)]
                       + _tile_scratch(COMBINE_TOKENS_IN_FLIGHT),
        compiler_params=_params(("arbitrary",), VMEM_LIMIT),
        name="peer_v",
    )(ids, w, h, tab)


def _ple_kernel(h_ref, p_ref, g_ref, wg_ref, wp_ref, gf_ref, o_ref, *, final):
    h = h_ref[...]
    gate = _sigmoid(jnp.dot(_rms(h, g_ref[...]).astype(BF16), wg_ref[...],
                            preferred_element_type=F32))
    out = h + gate * jnp.dot(p_ref[...].astype(BF16), wp_ref[...],
                             preferred_element_type=F32)
    if final:
        out = _rms(out, gf_ref[...])
    o_ref[...] = out


def ple(h, p, g, wg, wp, gf, final, tm=512):
    t = h.shape[0]
    tm = min(tm, t)
    return pl.pallas_call(
        functools.partial(_ple_kernel, final=final),
        grid=(t // tm,),
        in_specs=[pl.BlockSpec((tm, D_MODEL), lambda i: (i, 0)),
                  pl.BlockSpec((tm, PLE_DIM), lambda i: (i, 0)),
                  pl.BlockSpec((1, D_MODEL), lambda i: (0, 0)),
                  pl.BlockSpec(wg.shape, lambda i: (0, 0)),
                  pl.BlockSpec(wp.shape, lambda i: (0, 0)),
                  pl.BlockSpec((1, D_MODEL), lambda i: (0, 0))],
        out_specs=pl.BlockSpec((tm, D_MODEL), lambda i: (i, 0)),
        out_shape=jax.ShapeDtypeStruct((t, D_MODEL), F32),
        compiler_params=_params(("arbitrary",), VMEM_LIMIT),
        name="ple",
    )(h, p, g, wg, wp, gf)


def _permute_w_in(w):
    q, k, v, rest = (w[:, :D_MODEL], w[:, D_MODEL:D_MODEL + KV_W],
                     w[:, D_MODEL + KV_W:D_MODEL + 2 * KV_W], w[:, D_MODEL + 2 * KV_W:])
    return jnp.concatenate([q, rest, k, v], axis=1).astype(BF16)


def _pack_table(tab):
    t16 = lax.bitcast_convert_type(tab.astype(BF16), jnp.uint16).astype(jnp.uint32)
    words = t16[:, :ROW_WORDS] | (t16[:, ROW_WORDS:] << 16)
    return lax.bitcast_convert_type(words, jnp.int32).reshape(N_EXPERTS * SLAB, LANES)


def kernel(x, p, norm_mix, w_in, attn_sinks, conv_w, rec_conv_w, rec_conv_b, w_rgate, b_rgate, w_igate, b_igate, lru_lambda, w_branch, w_out, norm_ffn, w_peer_q, peer_sub_keys, peer_u, peer_v, norm_ple, w_ple_gate, w_ple_proj, norm_final):
    batch, seq, d = x.shape
    depth = w_in.shape[0]
    t = batch * seq
    row = lambda a: a.reshape(1, -1)
    h = x.reshape(t, d)
    for l in range(depth):
        z = norm_matmul(h, row(norm_mix[l]), _permute_w_in(w_in[l]))
        attn = attention(z, attn_sinks[l], batch, seq)
        conv, rec = conv_rec(z, conv_w[l], rec_conv_w[l], row(rec_conv_b[l]),
                             w_rgate[l].astype(BF16), w_igate[l].astype(BF16),
                             row(b_rgate[l]), row(b_igate[l]), row(lru_lambda[l]), batch, seq)
        h = merge(attn, conv, rec, z, h, w_branch[l].astype(BF16), w_out[l].astype(BF16))
        xn, ids, gates = route(h, row(norm_ffn[l]), w_peer_q[l].astype(BF16), peer_sub_keys[l])
        w = peer_scores(ids, xn, gates, _pack_table(peer_u[l]))
        h = peer_combine(ids, w, h, _pack_table(peer_v[l]))
        h = ple(h, p[l].reshape(t, -1), row(norm_ple[l]), w_ple_gate[l].astype(BF16),
                w_ple_proj[l].astype(BF16), row(norm_final), final=(l == depth - 1))
    return h.reshape(batch, seq, d)
```

```python
import functools

import jax
import jax.numpy as jnp
from jax import lax
from jax.experimental import pallas as pl
from jax.experimental.pallas import tpu as pltpu

F32 = jnp.float32
BF16 = jnp.bfloat16

D_MODEL = 1024
N_Q_HEADS = 16
N_KV_HEADS = 4
HEAD_DIM = 64
GROUP = N_Q_HEADS // N_KV_HEADS
ATTN_BLOCK = 128
KV_W = N_KV_HEADS * HEAD_DIM
REC_HEADS = 4
REC_HEAD_DIM = D_MODEL // REC_HEADS
LRU_C = 8.0
N_KEYS = 128
N_EXPERTS = N_KEYS * N_KEYS
PEER_HEADS = 8
PEER_TOPK = 16
PEER_HALF = 128
N_PAIRS = PEER_HEADS * PEER_TOPK
PLE_DIM = 256
EPS = 1e-6
NEG_INF = -1e30
TAG_NONE = float(1 << 20)
ROUTE_HEADS_PER_STEP = 8

COL_Q, COL_CB, COL_CC, COL_CX, COL_RX, COL_RY, COL_G0 = 0, 1, 2, 3, 4, 5, 6
IN_COLS = 9 * D_MODEL + 2 * KV_W
COL_K_KV = 9 * D_MODEL // KV_W
COL_V_KV = COL_K_KV + 1

LANES = 128
SLAB = D_MODEL // 2 // LANES
ROW_WORDS = SLAB * LANES
TILE_STRIDE = N_PAIRS + 8
SCORES_TOKENS_IN_FLIGHT = 16
COMBINE_TOKENS_IN_FLIGHT = 16
VMEM_LIMIT = 56 * 1024 * 1024


def _params(sem, vmem=None):
    return pltpu.CompilerParams(dimension_semantics=sem, vmem_limit_bytes=vmem)


def _rms(x32, g):
    ms = jnp.mean(x32 * x32, axis=-1, keepdims=True)
    return x32 * lax.rsqrt(ms + EPS) * g


def _sigmoid(x):
    return 0.5 * jnp.tanh(0.5 * x) + 0.5


def _gelu(x):
    return 0.5 * x * (1.0 + jnp.tanh(0.7978845608028654 * (x + 0.044715 * (x * x * x))))


def _split_bf16(x32):
    hi = x32.astype(BF16)
    lo = (x32 - hi.astype(F32)).astype(BF16)
    return hi, lo


def _norm_matmul_kernel(x_ref, g_ref, w_ref, o_ref, xn_ref):
    @pl.when(pl.program_id(1) == 0)
    def _():
        xn_ref[...] = _rms(x_ref[...], g_ref[...]).astype(BF16)

    o_ref[...] = jnp.dot(xn_ref[...], w_ref[...],
                         preferred_element_type=F32).astype(o_ref.dtype)


def norm_matmul(x, g, w, tm=2048, tn=512):
    m, d = x.shape
    n = w.shape[1]
    tm = min(tm, m)
    return pl.pallas_call(
        _norm_matmul_kernel,
        grid=(m // tm, n // tn),
        in_specs=[pl.BlockSpec((tm, d), lambda i, j: (i, 0)),
                  pl.BlockSpec((1, d), lambda i, j: (0, 0)),
                  pl.BlockSpec((d, tn), lambda i, j: (0, j))],
        out_specs=pl.BlockSpec((tm, tn), lambda i, j: (i, j)),
        out_shape=jax.ShapeDtypeStruct((m, n), BF16),
        scratch_shapes=[pltpu.VMEM((tm, d), BF16)],
        compiler_params=_params(("arbitrary", "arbitrary"), VMEM_LIMIT),
        name="norm_matmul",
    )(x, g, w)


def _attn_kernel(sink_ref, q_ref, kc_ref, vc_ref, kp_ref, vp_ref, o_ref):
    n = pl.program_id(1)
    q = q_ref[...] * (HEAD_DIM ** -0.5)
    k = jnp.concatenate([kp_ref[...], kc_ref[...]], axis=0)
    v = jnp.concatenate([vp_ref[...], vc_ref[...]], axis=0)
    shape = (GROUP * ATTN_BLOCK, 2 * ATTN_BLOCK)
    row = lax.broadcasted_iota(jnp.int32, shape, 0)
    ki = lax.broadcasted_iota(jnp.int32, shape, 1)
    diff = (row & (ATTN_BLOCK - 1)) + ATTN_BLOCK - ki
    first = jnp.where(n > 0, 0, ATTN_BLOCK)
    inside = jnp.where(diff >= 0, jnp.where(diff < ATTN_BLOCK, ki - first, -1), -1)
    mask = inside >= 0
    grp = lax.broadcasted_iota(jnp.int32, (GROUP * ATTN_BLOCK, 1), 0) // ATTN_BLOCK
    ones = jnp.ones((2 * ATTN_BLOCK, HEAD_DIM), BF16)
    outs = []
    for h in range(N_KV_HEADS):
        kh = k[:, h * HEAD_DIM:(h + 1) * HEAD_DIM]
        vh = jnp.concatenate([v[:, h * HEAD_DIM:(h + 1) * HEAD_DIM], ones], axis=-1)
        qh = jnp.concatenate([q[:, (h * GROUP + g) * HEAD_DIM:(h * GROUP + g + 1) * HEAD_DIM]
                              for g in range(GROUP)], axis=0)
        sink = jnp.zeros((GROUP * ATTN_BLOCK, 1), F32)
        for g in range(GROUP):
            sink = jnp.where(grp == g, sink_ref[h * GROUP + g], sink)
        s = lax.dot_general(qh, kh, (((1,), (1,)), ((), ())), preferred_element_type=F32)
        s = jnp.where(mask, s, NEG_INF)
        m = jnp.maximum(jnp.max(s, axis=-1, keepdims=True), sink)
        e = jnp.exp(s - m).astype(BF16)
        ov = jnp.dot(e, vh, preferred_element_type=F32)
        denom = ov[:, HEAD_DIM:HEAD_DIM + 1] + jnp.exp(sink - m)
        o = ov[:, :HEAD_DIM] / denom
        outs.extend(o[g * ATTN_BLOCK:(g + 1) * ATTN_BLOCK, :] for g in range(GROUP))
    o_ref[...] = jnp.concatenate(outs, axis=-1).astype(o_ref.dtype)


def attention(z, sinks, batch, seq):
    nb = seq // ATTN_BLOCK
    t = batch * seq
    cur = lambda b, n: b * nb + n
    prev = lambda b, n: b * nb + jnp.maximum(n - 1, 0)
    return pl.pallas_call(
        _attn_kernel,
        grid=(batch, nb),
        in_specs=[pl.BlockSpec(memory_space=pltpu.SMEM),
                  pl.BlockSpec((ATTN_BLOCK, D_MODEL), lambda b, n: (cur(b, n), COL_Q)),
                  pl.BlockSpec((ATTN_BLOCK, KV_W), lambda b, n: (cur(b, n), COL_K_KV)),
                  pl.BlockSpec((ATTN_BLOCK, KV_W), lambda b, n: (cur(b, n), COL_V_KV)),
                  pl.BlockSpec((ATTN_BLOCK, KV_W), lambda b, n: (prev(b, n), COL_K_KV)),
                  pl.BlockSpec((ATTN_BLOCK, KV_W), lambda b, n: (prev(b, n), COL_V_KV))],
        out_specs=pl.BlockSpec((ATTN_BLOCK, D_MODEL), lambda b, n: (cur(b, n), 0)),
        out_shape=jax.ShapeDtypeStruct((t, D_MODEL), BF16),
        compiler_params=_params(("arbitrary", "arbitrary")),
        name="swa_attention",
    )(sinks, z, z, z, z, z)


HALO = 8


def _convrec_kernel(cb_ref, cc_ref, cx_ref, rx_ref, ry_ref, cw_ref, rcw_ref, rcb_ref,
                    wr_ref, wi_ref, br_ref, bi_ref, lam_ref, conv_ref, rec_ref,
                    ubuf, xbuf, hcar):
    ts = cb_ref.shape[0]

    @pl.when(pl.program_id(1) == 0)
    def _():
        ubuf[0:HALO, :] = jnp.zeros((HALO, D_MODEL), F32)
        xbuf[0:HALO, :] = jnp.zeros((HALO, D_MODEL), F32)
        hcar[...] = jnp.zeros_like(hcar)

    u = cc_ref[...].astype(F32) * cx_ref[...].astype(F32)
    ubuf[HALO:HALO + ts, :] = u
    y = (cw_ref[2:3, :] * u + cw_ref[1:2, :] * ubuf[HALO - 1:HALO - 1 + ts, :]
         + cw_ref[0:1, :] * ubuf[HALO - 2:HALO - 2 + ts, :])
    conv_ref[...] = (cb_ref[...].astype(F32) * y).astype(conv_ref.dtype)
    ubuf[0:HALO, :] = ubuf[ts:ts + HALO, :]

    x = rx_ref[...].astype(F32)
    xbuf[HALO:HALO + ts, :] = x
    c = (rcw_ref[3:4, :] * x + rcw_ref[2:3, :] * xbuf[HALO - 1:HALO - 1 + ts, :]
         + rcw_ref[1:2, :] * xbuf[HALO - 2:HALO - 2 + ts, :]
         + rcw_ref[0:1, :] * xbuf[HALO - 3:HALO - 3 + ts, :] + rcb_ref[...])
    xbuf[0:HALO, :] = xbuf[ts:ts + HALO, :]
    cbf = c.astype(BF16)
    r_parts, i_parts = [], []
    for h in range(REC_HEADS):
        ch = cbf[:, h * REC_HEAD_DIM:(h + 1) * REC_HEAD_DIM]
        r_parts.append(jnp.dot(ch, wr_ref[h], preferred_element_type=F32))
        i_parts.append(jnp.dot(ch, wi_ref[h], preferred_element_type=F32))
    r = _sigmoid(jnp.concatenate(r_parts, axis=-1) + br_ref[...])
    ig = _sigmoid(jnp.concatenate(i_parts, axis=-1) + bi_ref[...])
    nl = -lam_ref[...]
    softplus = jnp.maximum(nl, 0.0) + jnp.log(1.0 + jnp.exp(-jnp.abs(nl)))
    log_a = -LRU_C * r * softplus
    a = jnp.exp(log_a)
    b = jnp.sqrt(1.0 - jnp.exp(2.0 * log_a)) * (ig * c)
    a3 = a.reshape(ts // 8, 8, D_MODEL)
    b3 = b.reshape(ts // 8, 8, D_MODEL)
    rid = lax.broadcasted_iota(jnp.int32, a3.shape, 1)
    for d in (1, 2, 4):
        inside = rid >= d
        a_sh = jnp.where(inside, pltpu.roll(a3, d, axis=1), 1.0)
        b_sh = jnp.where(inside, pltpu.roll(b3, d, axis=1), 0.0)
        b3 = a3 * b_sh + b3
        a3 = a3 * a_sh
    carry = hcar[0:1, :]
    groups = []
    for g in range(ts // 8):
        hg = b3[g] + a3[g] * carry
        groups.append(hg)
        carry = hg[7:8, :]
    hcar[0:1, :] = carry
    hs = jnp.concatenate(groups, axis=0)
    rec_ref[...] = (_gelu(ry_ref[...].astype(F32)) * hs).astype(rec_ref.dtype)


def conv_rec(z, cw, rcw, rcb, wr, wi, br, bi, lam, batch, seq, ts=256):
    ts = min(ts, seq)
    nt = seq // ts
    t = batch * seq
    row = lambda b, i: b * nt + i
    zspec = lambda col: pl.BlockSpec((ts, D_MODEL), lambda b, i: (row(b, i), col))
    full = lambda a: pl.BlockSpec(a.shape, lambda b, i: (0,) * a.ndim)
    return pl.pallas_call(
        _convrec_kernel,
        grid=(batch, nt),
        in_specs=[zspec(COL_CB), zspec(COL_CC), zspec(COL_CX), zspec(COL_RX), zspec(COL_RY),
                  full(cw), full(rcw), full(rcb), full(wr), full(wi), full(br), full(bi),
                  full(lam)],
        out_specs=[pl.BlockSpec((ts, D_MODEL), lambda b, i: (row(b, i), 0)),
                   pl.BlockSpec((ts, D_MODEL), lambda b, i: (row(b, i), 0))],
        out_shape=[jax.ShapeDtypeStruct((t, D_MODEL), BF16),
                   jax.ShapeDtypeStruct((t, D_MODEL), BF16)],
        scratch_shapes=[pltpu.VMEM((ts + HALO, D_MODEL), F32),
                        pltpu.VMEM((ts + HALO, D_MODEL), F32),
                        pltpu.VMEM((HALO, D_MODEL), F32)],
        compiler_params=_params(("arbitrary", "arbitrary"), VMEM_LIMIT),
        name="conv_rec",
    )(z, z, z, z, z, cw, rcw, rcb, wr, wi, br, bi, lam)


def _merge_kernel(a_ref, c_ref, r_ref, g0_ref, g1_ref, g2_ref, h_ref, wb_ref, wo_ref, o_ref):
    merged = None
    for n, (br, gt) in enumerate(((a_ref, g0_ref), (c_ref, g1_ref), (r_ref, g2_ref))):
        y = jnp.dot(br[...], wb_ref[n], preferred_element_type=F32)
        term = _sigmoid(gt[...].astype(F32)) * y
        merged = term if merged is None else merged + term
    o_ref[...] = h_ref[...] + jnp.dot(merged.astype(BF16), wo_ref[...],
                                      preferred_element_type=F32)


def merge(attn, conv, rec, z, h, wb, wo, tm=512):
    t = h.shape[0]
    tm = min(tm, t)
    blk = lambda col: pl.BlockSpec((tm, D_MODEL), lambda i: (i, col))
    return pl.pallas_call(
        _merge_kernel,
        grid=(t // tm,),
        in_specs=[blk(0), blk(0), blk(0), blk(COL_G0), blk(COL_G0 + 1), blk(COL_G0 + 2), blk(0),
                  pl.BlockSpec(wb.shape, lambda i: (0, 0, 0)),
                  pl.BlockSpec(wo.shape, lambda i: (0, 0))],
        out_specs=blk(0),
        out_shape=jax.ShapeDtypeStruct((t, D_MODEL), F32),
        compiler_params=_params(("arbitrary",), VMEM_LIMIT),
        name="merge",
    )(attn, conv, rec, z, z, z, h, wb, wo)


def _sorting_network(n):
    pairs = []
    p = 1
    while p < n:
        k = p
        while k >= 1:
            for j in range(k % p, n - k, 2 * k):
                for i in range(min(k, n - j - k)):
                    if (i + j) // (2 * p) == (i + j + k) // (2 * p):
                        pairs.append((i + j, i + j + k))
            k //= 2
        p *= 2
    return pairs


def _topk_columns(vals, tags, k):
    vals, tags = list(vals), list(tags)
    n = len(vals)
    for i, j in _sorting_network(n):
        va, ta, vb, tb = vals[i], tags[i], vals[j], tags[j]
        first = (va > vb) | ((va == vb) & (ta < tb))
        vals[i], vals[j] = jnp.where(first, va, vb), jnp.where(first, vb, va)
        tags[i], tags[j] = jnp.where(first, ta, tb), jnp.where(first, tb, ta)
    out_v, out_t = [], []
    for it in range(k):
        m = jnp.max(vals[0], axis=0, keepdims=True)
        best = jnp.min(jnp.where(vals[0] == m, tags[0], TAG_NONE), axis=0, keepdims=True)
        out_v.append(m)
        out_t.append(best)
        left = k - 1 - it
        pop = tags[0] == best
        for d in range(min(left, n - 1)):
            vals[d] = jnp.where(pop, vals[d + 1], vals[d])
            tags[d] = jnp.where(pop, tags[d + 1], tags[d])
        if left > n - 1:
            vals[n - 1] = jnp.where(pop, -jnp.inf, vals[n - 1])
            tags[n - 1] = jnp.where(pop, TAG_NONE, tags[n - 1])
    return out_v, out_t


def _stack_rows(rows, lo, hi):
    n = hi - lo
    rid = lax.broadcasted_iota(jnp.int32, (n, rows[0].shape[1]), 0)
    out = jnp.broadcast_to(rows[lo], (n, rows[0].shape[1]))
    for i in range(1, n):
        out = jnp.where(rid == i, rows[lo + i], out)
    return out


def _pair_candidates(s1, s2):
    k = PEER_TOPK
    tm = s1[0].shape[1]
    r = lax.broadcasted_iota(jnp.int32, (8, tm), 0)
    a_lo, a_hi = _stack_rows(s1, 0, 8), _stack_rows(s1, 8, 16)
    b_lo, b_hi = _stack_rows(s2, 0, 8), _stack_rows(s2, 8, 16)

    def slab(vals, tag, lo, hi):
        keep = jnp.where(r >= lo, r, hi + 1) <= hi
        return jnp.where(keep, vals, -jnp.inf), tag.astype(F32)

    slabs = [slab(a_lo + s2[0], r * k, 0, 7),
             slab(a_hi + s2[0], (r + 8) * k, 0, 7),
             slab(s1[0] + b_lo, r, 1, 7),
             slab(s1[0] + b_hi, r + 8, 0, 7),
             slab(a_lo + s2[1], r * k + 1, 1, 7),
             slab(s1[1] + b_lo, r + k, 2, 7),
             slab(a_lo + s2[2], r * k + 2, 2, 4),
             slab(a_lo + s2[3], r * k + 3, 2, 3),
             slab(a_lo + s2[4], r * k + 4, 2, 2)]
    return [v for v, _ in slabs], [t for _, t in slabs]


def _route_kernel(h_ref, g_ref, wq_ref, keys_ref, xn_ref, ids_ref, gates_ref,
                  q_ref, ids_t, gates_t):
    xn = _rms(h_ref[...], g_ref[...])
    xn_ref[...] = xn
    q_ref[...] = jnp.dot(xn.astype(BF16), wq_ref[...], preferred_element_type=F32)
    tm = h_ref.shape[0]
    row8 = lax.broadcasted_iota(jnp.int32, (8, tm), 0)
    key_tags = [(row8 + 8 * g).astype(F32) for g in range(N_KEYS // 8)]

    def one_head(hd):
        tops = []
        for p in range(2):
            c0 = pl.multiple_of((hd * 2 + p) * PEER_HALF, PEER_HALF)
            sc = lax.dot_general(keys_ref[p], q_ref[:, pl.ds(c0, PEER_HALF)],
                                 (((1,), (1,)), ((), ())),
                                 precision=lax.Precision.HIGHEST,
                                 preferred_element_type=F32)
            slabs = [sc[8 * g:8 * g + 8, :] for g in range(N_KEYS // 8)]
            tops.append(_topk_columns(slabs, key_tags, PEER_TOPK))
        (s1, i1), (s2, i2) = tops
        cand, flat = _pair_candidates(s1, s2)
        best_s, best_c = _topk_columns(cand, flat, PEER_TOPK)
        best_s = _stack_rows(best_s, 0, PEER_TOPK)
        flat_c = _stack_rows(best_c, 0, PEER_TOPK).astype(jnp.int32)
        a_idx = flat_c >> 4
        b_idx = flat_c & (PEER_TOPK - 1)
        e1 = jnp.zeros(flat_c.shape, F32)
        e2 = jnp.zeros(flat_c.shape, F32)
        for a in range(PEER_TOPK):
            e1 = jnp.where(a_idx == a, i1[a], e1)
            e2 = jnp.where(b_idx == a, i2[a], e2)
        ex = jnp.exp(best_s - best_s[0:1, :])
        r0 = pl.multiple_of(hd * PEER_TOPK, PEER_TOPK)
        gates_t[pl.ds(r0, PEER_TOPK), :] = ex / jnp.sum(ex, axis=0, keepdims=True)
        ids_t[pl.ds(r0, PEER_TOPK), :] = ((e1 * N_KEYS + e2) * SLAB).astype(jnp.int32)

    def heads(step, carry):
        for i in range(ROUTE_HEADS_PER_STEP):
            one_head(step * ROUTE_HEADS_PER_STEP + i)
        return carry

    lax.fori_loop(0, PEER_HEADS // ROUTE_HEADS_PER_STEP, heads, 0)
    ids_ref[...] = ids_t[...].T
    gates_ref[...] = gates_t[...].T


def route(h, g, wq, keys, tm=128):
    t = h.shape[0]
    return pl.pallas_call(
        _route_kernel,
        grid=(t // tm,),
        in_specs=[pl.BlockSpec((tm, D_MODEL), lambda i: (i, 0)),
                  pl.BlockSpec((1, D_MODEL), lambda i: (0, 0)),
                  pl.BlockSpec(wq.shape, lambda i: (0, 0)),
                  pl.BlockSpec(keys.shape, lambda i: (0, 0, 0))],
        out_specs=[pl.BlockSpec((tm, D_MODEL), lambda i: (i, 0)),
                   pl.BlockSpec((tm, N_PAIRS), lambda i: (i, 0)),
                   pl.BlockSpec((tm, N_PAIRS), lambda i: (i, 0))],
        out_shape=[jax.ShapeDtypeStruct((t, D_MODEL), F32),
                   jax.ShapeDtypeStruct((t, N_PAIRS), jnp.int32),
                   jax.ShapeDtypeStruct((t, N_PAIRS), F32)],
        scratch_shapes=[pltpu.VMEM((tm, 2 * PEER_HALF * PEER_HEADS), F32),
                        pltpu.VMEM((N_PAIRS, tm), jnp.int32),
                        pltpu.VMEM((N_PAIRS, tm), F32)],
        compiler_params=_params(("arbitrary",), VMEM_LIMIT),
        name="peer_route",
    )(h, g, wq, keys)


def _gather_tokens(ids_ref, tab_ref, tiles, t0):
    zero = lax.min(t0, 0)
    offs = [zero + u for u in range(8)]
    for jj in range(N_PAIRS // 8):
        rows = [ids_ref.at[t0 + q, pl.ds(8 * jj, 8)] for q in range(len(tiles))]
        for u in range(8):
            j = 8 * jj + u
            for q, tile in enumerate(tiles):
                e = pl.multiple_of(rows[q][offs[u]], SLAB)
                tile[pl.ds(j, SLAB, stride=TILE_STRIDE), :] = tab_ref[pl.ds(e, SLAB), :]


def _pipelined_tokens(tb, tiles, consume, ids_ref, tab_ref):
    nfl = len(tiles)

    def body(k, carry):
        t0 = nfl * k
        for q in range(nfl):
            consume(jnp.maximum(t0 - nfl + q, 0), tiles[q])
        _gather_tokens(ids_ref, tab_ref, tiles, t0)
        return carry

    lax.fori_loop(0, tb // nfl, body, 0)
    for q in range(nfl):
        consume(tb - nfl + q, tiles[q])


def _tile_chunk(tile, s):
    return pltpu.bitcast(tile[pl.ds(s * TILE_STRIDE, N_PAIRS), :], BF16)


def _lane_parity(shape):
    lane = lax.broadcasted_iota(jnp.int32, shape, len(shape) - 1)
    return (lane & 1) == 1


def _peer_u_kernel(ids_ref, x_ref, gates_ref, tab_ref, w_ref, sc_ref, *tiles):
    tb = x_ref.shape[0]

    @pl.when(pl.program_id(0) == 0)
    def _():
        for tile in tiles:
            tile[...] = jnp.zeros_like(tile)

    def scores(t, tile):
        g = jnp.concatenate([_tile_chunk(tile, s) for s in range(SLAB)], axis=-1)
        xr = x_ref[pl.ds(t, 1), :]
        xa_hi, xa_lo = _split_bf16(xr[:, :ROW_WORDS])
        xb_hi, xb_lo = _split_bf16(xr[:, ROW_WORDS:])
        x8 = jnp.concatenate([xa_hi, xb_hi, xa_lo, xb_lo,
                              jnp.zeros((4, ROW_WORDS), BF16)], axis=0)
        res = lax.dot_general(x8, g, (((1,), (1,)), ((), ())),
                              preferred_element_type=F32)
        odd = _lane_parity((1, 2 * N_PAIRS))
        part = jnp.where(odd, res[1:2] + res[3:4], res[0:1] + res[2:3])
        sc_ref[pl.ds(t, 1), :] = part + pltpu.roll(part, 1, axis=1)

    _pipelined_tokens(tb, tiles, scores, ids_ref, tab_ref)

    r = lax.broadcasted_iota(jnp.int32, (N_PAIRS, 2 * N_PAIRS), 0)
    c = lax.broadcasted_iota(jnp.int32, (N_PAIRS, 2 * N_PAIRS), 1)
    spread = jnp.where(c == 2 * r + 1, 1.0, 0.0).astype(BF16)
    g_hi, g_lo = _split_bf16(gates_ref[...])
    g2 = (jnp.dot(g_hi, spread, preferred_element_type=F32)
          + jnp.dot(g_lo, spread, preferred_element_type=F32))
    w_ref[...] = g2 * _gelu(sc_ref[...])


def _table_spec():
    return pl.BlockSpec((N_EXPERTS * SLAB, LANES), lambda i: (0, 0),
                        pipeline_mode=pl.Buffered(1))


def _tile_scratch(n):
    return [pltpu.VMEM((SLAB * TILE_STRIDE, LANES), jnp.int32) for _ in range(n)]


def peer_scores(ids, xn, gates, tab, tb=512):
    t = xn.shape[0]
    return pl.pallas_call(
        _peer_u_kernel,
        grid=(t // tb,),
        in_specs=[pl.BlockSpec((tb, N_PAIRS), lambda i: (i, 0), memory_space=pltpu.SMEM),
                  pl.BlockSpec((tb, D_MODEL), lambda i: (i, 0)),
                  pl.BlockSpec((tb, N_PAIRS), lambda i: (i, 0)),
                  _table_spec()],
        out_specs=pl.BlockSpec((tb, 2 * N_PAIRS), lambda i: (i, 0)),
        out_shape=jax.ShapeDtypeStruct((t, 2 * N_PAIRS), F32),
        scratch_shapes=([pltpu.VMEM((tb, 2 * N_PAIRS), F32)]
                        + _tile_scratch(SCORES_TOKENS_IN_FLIGHT)),
        compiler_params=_params(("arbitrary",), VMEM_LIMIT),
        name="peer_u",
    )(ids, xn, gates, tab)


def _peer_v_kernel(ids_ref, w_ref, h_ref, tab_ref, o_ref, acc_ref, lhs_refs, *tiles):
    tb = h_ref.shape[0]

    @pl.when(pl.program_id(0) == 0)
    def _():
        for tile in tiles:
            tile[...] = jnp.zeros_like(tile)

    w_odd = w_ref[...]
    w_even = pltpu.roll(w_odd, 2 * N_PAIRS - 1, axis=1)
    for i, w in enumerate((w_even, w_odd)):
        hi = w.astype(BF16).astype(F32)
        lhs_refs[i, :, :] = hi
        lhs_refs[2 + i, :, :] = w - hi

    def combine(t, tile):
        rid = lax.broadcasted_iota(jnp.int32, (8, 2 * N_PAIRS), 0)
        lhs = jnp.zeros((8, 2 * N_PAIRS), F32)
        for i in range(4):
            row = lhs_refs[i, pl.ds(t, 1), :]
            lhs = jnp.where(rid == i, row, lhs)
        lhs = lhs.astype(BF16)
        lo, hi = [], []
        for s in range(SLAB):
            res = jnp.dot(lhs, _tile_chunk(tile, s), preferred_element_type=F32)
            lo.append(res[0:1] + res[2:3])
            hi.append(res[1:2] + res[3:4])
        acc_ref[pl.ds(t, 1), :] = jnp.concatenate(lo + hi, axis=-1)

    _pipelined_tokens(tb, tiles, combine, ids_ref, tab_ref)
    o_ref[...] = h_ref[...] + acc_ref[...]


def peer_combine(ids, w, h, tab, tb=512):
    t = h.shape[0]
    return pl.pallas_call(
        _peer_v_kernel,
        grid=(t // tb,),
        in_specs=[pl.BlockSpec((tb, N_PAIRS), lambda i: (i, 0), memory_space=pltpu.SMEM),
                  pl.BlockSpec((tb, 2 * N_PAIRS), lambda i: (i, 0)),
                  pl.BlockSpec((tb, D_MODEL), lambda i: (i, 0)),
                  _table_spec()],
        out_specs=pl.BlockSpec((tb, D_MODEL), lambda i: (i, 0)),
        out_shape=jax.ShapeDtypeStruct((t, D_MODEL), F32),
        scratch_shapes=[pltpu.VMEM((tb, D_MODEL), F32),
                        pltpu.VMEM((4, tb, 2 * N_PAIRS), F32)]
                       + _tile_scratch(COMBINE_TOKENS_IN_FLIGHT),
        compiler_params=_params(("arbitrary",), VMEM_LIMIT),
        name="peer_v",
    )(ids, w, h, tab)


def _ple_kernel(h_ref, p_ref, g_ref, wg_ref, wp_ref, gf_ref, o_ref, *, final):
    h = h_ref[...]
    gate = _sigmoid(jnp.dot(_rms(h, g_ref[...]).astype(BF16), wg_ref[...],
                            preferred_element_type=F32))
    out = h + gate * jnp.dot(p_ref[...].astype(BF16), wp_ref[...],
                             preferred_element_type=F32)
    if final:
        out = _rms(out, gf_ref[...])
    o_ref[...] = out


def ple(h, p, layer, g, wg, wp, gf, final, tm=512):
    t = h.shape[0]
    tm = min(tm, t)
    return pl.pallas_call(
        functools.partial(_ple_kernel, final=final),
        grid=(t // tm,),
        in_specs=[pl.BlockSpec((tm, D_MODEL), lambda i: (i, 0)),
                  pl.BlockSpec((None, tm, PLE_DIM), lambda i: (layer, i, 0)),
                  pl.BlockSpec((1, D_MODEL), lambda i: (0, 0)),
                  pl.BlockSpec(wg.shape, lambda i: (0, 0)),
                  pl.BlockSpec(wp.shape, lambda i: (0, 0)),
                  pl.BlockSpec((1, D_MODEL), lambda i: (0, 0))],
        out_specs=pl.BlockSpec((tm, D_MODEL), lambda i: (i, 0)),
        out_shape=jax.ShapeDtypeStruct((t, D_MODEL), F32),
        compiler_params=_params(("arbitrary",), VMEM_LIMIT),
        name="ple",
    )(h, p, g, wg, wp, gf)


def _permute_w_in(w):
    q, k, v, rest = (w[:, :D_MODEL], w[:, D_MODEL:D_MODEL + KV_W],
                     w[:, D_MODEL + KV_W:D_MODEL + 2 * KV_W], w[:, D_MODEL + 2 * KV_W:])
    return jnp.concatenate([q, rest, k, v], axis=1).astype(BF16)


def _pack_table(tab):
    t16 = lax.bitcast_convert_type(tab.astype(BF16), jnp.uint16).astype(jnp.uint32)
    words = t16[:, :ROW_WORDS] | (t16[:, ROW_WORDS:] << 16)
    return lax.bitcast_convert_type(words, jnp.int32).reshape(N_EXPERTS * SLAB, LANES)


def kernel(x, p, norm_mix, w_in, attn_sinks, conv_w, rec_conv_w, rec_conv_b, w_rgate, b_rgate, w_igate, b_igate, lru_lambda, w_branch, w_out, norm_ffn, w_peer_q, peer_sub_keys, peer_u, peer_v, norm_ple, w_ple_gate, w_ple_proj, norm_final):
    batch, seq, d = x.shape
    depth = w_in.shape[0]
    t = batch * seq
    row = lambda a: a.reshape(1, -1)
    h = x.reshape(t, d)
    p3 = p.reshape(depth, t, -1)
    for l in range(depth):
        z = norm_matmul(h, row(norm_mix[l]), _permute_w_in(w_in[l]))
        attn = attention(z, attn_sinks[l], batch, seq)
        conv, rec = conv_rec(z, conv_w[l], rec_conv_w[l], row(rec_conv_b[l]),
                             w_rgate[l].astype(BF16), w_igate[l].astype(BF16),
                             row(b_rgate[l]), row(b_igate[l]), row(lru_lambda[l]), batch, seq)
        h = merge(attn, conv, rec, z, h, w_branch[l].astype(BF16), w_out[l].astype(BF16))
        xn, ids, gates = route(h, row(norm_ffn[l]), w_peer_q[l].astype(BF16), peer_sub_keys[l])
        w = peer_scores(ids, xn, gates, _pack_table(peer_u[l]))
        h = peer_combine(ids, w, h, _pack_table(peer_v[l]))
        h = ple(h, p3, l, row(norm_ple[l]), w_ple_gate[l].astype(BF16),
                w_ple_proj[l].astype(BF16), row(norm_final), final=(l == depth - 1))
    return h.reshape(batch, seq, d)
```

```python
import functools

import jax
import jax.numpy as jnp
from jax import lax
from jax.experimental import pallas as pl
from jax.experimental.pallas import tpu as pltpu

F32 = jnp.float32
BF16 = jnp.bfloat16

D_MODEL = 1024
N_Q_HEADS = 16
N_KV_HEADS = 4
HEAD_DIM = 64
GROUP = N_Q_HEADS // N_KV_HEADS
ATTN_BLOCK = 128
KV_W = N_KV_HEADS * HEAD_DIM
REC_HEADS = 4
REC_HEAD_DIM = D_MODEL // REC_HEADS
LRU_C = 8.0
N_KEYS = 128
N_EXPERTS = N_KEYS * N_KEYS
PEER_HEADS = 8
PEER_TOPK = 16
PEER_HALF = 128
N_PAIRS = PEER_HEADS * PEER_TOPK
PLE_DIM = 256
EPS = 1e-6
NEG_INF = -1e30
TAG_NONE = float(1 << 20)
ROUTE_HEADS_PER_STEP = 8

COL_Q, COL_CB, COL_CC, COL_CX, COL_RX, COL_RY, COL_G0 = 0, 1, 2, 3, 4, 5, 6
IN_COLS = 9 * D_MODEL + 2 * KV_W
COL_K_KV = 9 * D_MODEL // KV_W
COL_V_KV = COL_K_KV + 1

LANES = 128
SLAB = D_MODEL // 2 // LANES
ROW_WORDS = SLAB * LANES
TILE_STRIDE = N_PAIRS + 8
SCORES_TOKENS_IN_FLIGHT = 16
COMBINE_TOKENS_IN_FLIGHT = 16
VMEM_LIMIT = 56 * 1024 * 1024


def _params(sem, vmem=None):
    return pltpu.CompilerParams(dimension_semantics=sem, vmem_limit_bytes=vmem)


def _rms(x32, g):
    ms = jnp.mean(x32 * x32, axis=-1, keepdims=True)
    return x32 * lax.rsqrt(ms + EPS) * g


def _sigmoid(x):
    return 0.5 * jnp.tanh(0.5 * x) + 0.5


def _gelu(x):
    return 0.5 * x * (1.0 + jnp.tanh(0.7978845608028654 * (x + 0.044715 * (x * x * x))))


def _split_bf16(x32):
    hi = x32.astype(BF16)
    lo = (x32 - hi.astype(F32)).astype(BF16)
    return hi, lo


def _norm_matmul_kernel(x_ref, g_ref, w_ref, o_ref, xn_ref):
    @pl.when(pl.program_id(1) == 0)
    def _():
        xn_ref[...] = _rms(x_ref[...], g_ref[...]).astype(BF16)

    o_ref[...] = jnp.dot(xn_ref[...], w_ref[...],
                         preferred_element_type=F32).astype(o_ref.dtype)


def norm_matmul(x, g, w, tm=2048, tn=1024):
    m, d = x.shape
    n = w.shape[1]
    tm = min(tm, m)
    return pl.pallas_call(
        _norm_matmul_kernel,
        grid=(m // tm, pl.cdiv(n, tn)),
        in_specs=[pl.BlockSpec((tm, d), lambda i, j: (i, 0)),
                  pl.BlockSpec((1, d), lambda i, j: (0, 0)),
                  pl.BlockSpec((d, tn), lambda i, j: (0, j))],
        out_specs=pl.BlockSpec((tm, tn), lambda i, j: (i, j)),
        out_shape=jax.ShapeDtypeStruct((m, n), BF16),
        scratch_shapes=[pltpu.VMEM((tm, d), BF16)],
        compiler_params=_params(("arbitrary", "arbitrary"), VMEM_LIMIT),
        name="norm_matmul",
    )(x, g, w)


def _attn_kernel(sink_ref, q_ref, kc_ref, vc_ref, kp_ref, vp_ref, o_ref):
    n = pl.program_id(1)
    q = q_ref[...] * (HEAD_DIM ** -0.5)
    k = jnp.concatenate([kp_ref[...], kc_ref[...]], axis=0)
    v = jnp.concatenate([vp_ref[...], vc_ref[...]], axis=0)
    shape = (GROUP * ATTN_BLOCK, 2 * ATTN_BLOCK)
    row = lax.broadcasted_iota(jnp.int32, shape, 0)
    ki = lax.broadcasted_iota(jnp.int32, shape, 1)
    diff = (row & (ATTN_BLOCK - 1)) + ATTN_BLOCK - ki
    first = jnp.where(n > 0, 0, ATTN_BLOCK)
    inside = jnp.where(diff >= 0, jnp.where(diff < ATTN_BLOCK, ki - first, -1), -1)
    mask = inside >= 0
    grp = lax.broadcasted_iota(jnp.int32, (GROUP * ATTN_BLOCK, 1), 0) // ATTN_BLOCK
    ones = jnp.ones((2 * ATTN_BLOCK, HEAD_DIM), BF16)
    outs = []
    for h in range(N_KV_HEADS):
        kh = k[:, h * HEAD_DIM:(h + 1) * HEAD_DIM]
        vh = jnp.concatenate([v[:, h * HEAD_DIM:(h + 1) * HEAD_DIM], ones], axis=-1)
        qh = jnp.concatenate([q[:, (h * GROUP + g) * HEAD_DIM:(h * GROUP + g + 1) * HEAD_DIM]
                              for g in range(GROUP)], axis=0)
        sink = jnp.zeros((GROUP * ATTN_BLOCK, 1), F32)
        for g in range(GROUP):
            sink = jnp.where(grp == g, sink_ref[h * GROUP + g], sink)
        s = lax.dot_general(qh, kh, (((1,), (1,)), ((), ())), preferred_element_type=F32)
        s = jnp.where(mask, s, NEG_INF)
        m = jnp.maximum(jnp.max(s, axis=-1, keepdims=True), sink)
        e = jnp.exp(s - m).astype(BF16)
        ov = jnp.dot(e, vh, preferred_element_type=F32)
        denom = ov[:, HEAD_DIM:HEAD_DIM + 1] + jnp.exp(sink - m)
        o = ov[:, :HEAD_DIM] / denom
        outs.extend(o[g * ATTN_BLOCK:(g + 1) * ATTN_BLOCK, :] for g in range(GROUP))
    o_ref[...] = jnp.concatenate(outs, axis=-1).astype(o_ref.dtype)


def attention(z, sinks, batch, seq):
    nb = seq // ATTN_BLOCK
    t = batch * seq
    cur = lambda b, n: b * nb + n
    prev = lambda b, n: b * nb + jnp.maximum(n - 1, 0)
    return pl.pallas_call(
        _attn_kernel,
        grid=(batch, nb),
        in_specs=[pl.BlockSpec(memory_space=pltpu.SMEM),
                  pl.BlockSpec((ATTN_BLOCK, D_MODEL), lambda b, n: (cur(b, n), COL_Q)),
                  pl.BlockSpec((ATTN_BLOCK, KV_W), lambda b, n: (cur(b, n), COL_K_KV)),
                  pl.BlockSpec((ATTN_BLOCK, KV_W), lambda b, n: (cur(b, n), COL_V_KV)),
                  pl.BlockSpec((ATTN_BLOCK, KV_W), lambda b, n: (prev(b, n), COL_K_KV)),
                  pl.BlockSpec((ATTN_BLOCK, KV_W), lambda b, n: (prev(b, n), COL_V_KV))],
        out_specs=pl.BlockSpec((ATTN_BLOCK, D_MODEL), lambda b, n: (cur(b, n), 0)),
        out_shape=jax.ShapeDtypeStruct((t, D_MODEL), BF16),
        compiler_params=_params(("arbitrary", "arbitrary")),
        name="swa_attention",
    )(sinks, z, z, z, z, z)


HALO = 8


def _convrec_kernel(cb_ref, cc_ref, cx_ref, rx_ref, ry_ref, cw_ref, rcw_ref, rcb_ref,
                    wr_ref, wi_ref, br_ref, bi_ref, lam_ref, conv_ref, rec_ref,
                    ubuf, xbuf, hcar):
    ts = cb_ref.shape[0]

    @pl.when(pl.program_id(1) == 0)
    def _():
        ubuf[0:HALO, :] = jnp.zeros((HALO, D_MODEL), F32)
        xbuf[0:HALO, :] = jnp.zeros((HALO, D_MODEL), F32)
        hcar[...] = jnp.zeros_like(hcar)

    u = cc_ref[...].astype(F32) * cx_ref[...].astype(F32)
    ubuf[HALO:HALO + ts, :] = u
    y = (cw_ref[2:3, :] * u + cw_ref[1:2, :] * ubuf[HALO - 1:HALO - 1 + ts, :]
         + cw_ref[0:1, :] * ubuf[HALO - 2:HALO - 2 + ts, :])
    conv_ref[...] = (cb_ref[...].astype(F32) * y).astype(conv_ref.dtype)
    ubuf[0:HALO, :] = ubuf[ts:ts + HALO, :]

    x = rx_ref[...].astype(F32)
    xbuf[HALO:HALO + ts, :] = x
    c = (rcw_ref[3:4, :] * x + rcw_ref[2:3, :] * xbuf[HALO - 1:HALO - 1 + ts, :]
         + rcw_ref[1:2, :] * xbuf[HALO - 2:HALO - 2 + ts, :]
         + rcw_ref[0:1, :] * xbuf[HALO - 3:HALO - 3 + ts, :] + rcb_ref[...])
    xbuf[0:HALO, :] = xbuf[ts:ts + HALO, :]
    cbf = c.astype(BF16)
    r_parts, i_parts = [], []
    for h in range(REC_HEADS):
        ch = cbf[:, h * REC_HEAD_DIM:(h + 1) * REC_HEAD_DIM]
        r_parts.append(jnp.dot(ch, wr_ref[h], preferred_element_type=F32))
        i_parts.append(jnp.dot(ch, wi_ref[h], preferred_element_type=F32))
    r = _sigmoid(jnp.concatenate(r_parts, axis=-1) + br_ref[...])
    ig = _sigmoid(jnp.concatenate(i_parts, axis=-1) + bi_ref[...])
    nl = -lam_ref[...]
    softplus = jnp.maximum(nl, 0.0) + jnp.log(1.0 + jnp.exp(-jnp.abs(nl)))
    log_a = -LRU_C * r * softplus
    a = jnp.exp(log_a)
    b = jnp.sqrt(1.0 - jnp.exp(2.0 * log_a)) * (ig * c)
    a3 = a.reshape(ts // 8, 8, D_MODEL)
    b3 = b.reshape(ts // 8, 8, D_MODEL)
    rid = lax.broadcasted_iota(jnp.int32, a3.shape, 1)
    for d in (1, 2, 4):
        inside = rid >= d
        a_sh = jnp.where(inside, pltpu.roll(a3, d, axis=1), 1.0)
        b_sh = jnp.where(inside, pltpu.roll(b3, d, axis=1), 0.0)
        b3 = a3 * b_sh + b3
        a3 = a3 * a_sh
    carry = hcar[0:1, :]
    groups = []
    for g in range(ts // 8):
        hg = b3[g] + a3[g] * carry
        groups.append(hg)
        carry = hg[7:8, :]
    hcar[0:1, :] = carry
    hs = jnp.concatenate(groups, axis=0)
    rec_ref[...] = (_gelu(ry_ref[...].astype(F32)) * hs).astype(rec_ref.dtype)


def conv_rec(z, cw, rcw, rcb, wr, wi, br, bi, lam, batch, seq, ts=256):
    ts = min(ts, seq)
    nt = seq // ts
    t = batch * seq
    row = lambda b, i: b * nt + i
    zspec = lambda col: pl.BlockSpec((ts, D_MODEL), lambda b, i: (row(b, i), col))
    full = lambda a: pl.BlockSpec(a.shape, lambda b, i: (0,) * a.ndim)
    return pl.pallas_call(
        _convrec_kernel,
        grid=(batch, nt),
        in_specs=[zspec(COL_CB), zspec(COL_CC), zspec(COL_CX), zspec(COL_RX), zspec(COL_RY),
                  full(cw), full(rcw), full(rcb), full(wr), full(wi), full(br), full(bi),
                  full(lam)],
        out_specs=[pl.BlockSpec((ts, D_MODEL), lambda b, i: (row(b, i), 0)),
                   pl.BlockSpec((ts, D_MODEL), lambda b, i: (row(b, i), 0))],
        out_shape=[jax.ShapeDtypeStruct((t, D_MODEL), BF16),
                   jax.ShapeDtypeStruct((t, D_MODEL), BF16)],
        scratch_shapes=[pltpu.VMEM((ts + HALO, D_MODEL), F32),
                        pltpu.VMEM((ts + HALO, D_MODEL), F32),
                        pltpu.VMEM((HALO, D_MODEL), F32)],
        compiler_params=_params(("arbitrary", "arbitrary"), VMEM_LIMIT),
        name="conv_rec",
    )(z, z, z, z, z, cw, rcw, rcb, wr, wi, br, bi, lam)


def _merge_kernel(a_ref, c_ref, r_ref, g0_ref, g1_ref, g2_ref, h_ref, wb_ref, wo_ref, o_ref):
    merged = None
    for n, (br, gt) in enumerate(((a_ref, g0_ref), (c_ref, g1_ref), (r_ref, g2_ref))):
        y = jnp.dot(br[...], wb_ref[n], preferred_element_type=F32)
        term = _sigmoid(gt[...].astype(F32)) * y
        merged = term if merged is None else merged + term
    o_ref[...] = h_ref[...] + jnp.dot(merged.astype(BF16), wo_ref[...],
                                      preferred_element_type=F32)


def merge(attn, conv, rec, z, h, wb, wo, tm=512):
    t = h.shape[0]
    tm = min(tm, t)
    blk = lambda col: pl.BlockSpec((tm, D_MODEL), lambda i: (i, col))
    return pl.pallas_call(
        _merge_kernel,
        grid=(t // tm,),
        in_specs=[blk(0), blk(0), blk(0), blk(COL_G0), blk(COL_G0 + 1), blk(COL_G0 + 2), blk(0),
                  pl.BlockSpec(wb.shape, lambda i: (0, 0, 0)),
                  pl.BlockSpec(wo.shape, lambda i: (0, 0))],
        out_specs=blk(0),
        out_shape=jax.ShapeDtypeStruct((t, D_MODEL), F32),
        compiler_params=_params(("arbitrary",), VMEM_LIMIT),
        name="merge",
    )(attn, conv, rec, z, z, z, h, wb, wo)


def _sorting_network(n):
    pairs = []
    p = 1
    while p < n:
        k = p
        while k >= 1:
            for j in range(k % p, n - k, 2 * k):
                for i in range(min(k, n - j - k)):
                    if (i + j) // (2 * p) == (i + j + k) // (2 * p):
                        pairs.append((i + j, i + j + k))
            k //= 2
        p *= 2
    return pairs


def _topk_columns(vals, tags, k):
    vals, tags = list(vals), list(tags)
    n = len(vals)
    for i, j in _sorting_network(n):
        va, ta, vb, tb = vals[i], tags[i], vals[j], tags[j]
        first = (va > vb) | ((va == vb) & (ta < tb))
        vals[i], vals[j] = jnp.where(first, va, vb), jnp.where(first, vb, va)
        tags[i], tags[j] = jnp.where(first, ta, tb), jnp.where(first, tb, ta)
    out_v, out_t = [], []
    for it in range(k):
        m = jnp.max(vals[0], axis=0, keepdims=True)
        best = jnp.min(jnp.where(vals[0] == m, tags[0], TAG_NONE), axis=0, keepdims=True)
        out_v.append(m)
        out_t.append(best)
        left = k - 1 - it
        pop = tags[0] == best
        for d in range(min(left, n - 1)):
            vals[d] = jnp.where(pop, vals[d + 1], vals[d])
            tags[d] = jnp.where(pop, tags[d + 1], tags[d])
        if left > n - 1:
            vals[n - 1] = jnp.where(pop, -jnp.inf, vals[n - 1])
            tags[n - 1] = jnp.where(pop, TAG_NONE, tags[n - 1])
    return out_v, out_t


def _stack_rows(rows, lo, hi):
    n = hi - lo
    rid = lax.broadcasted_iota(jnp.int32, (n, rows[0].shape[1]), 0)
    out = jnp.broadcast_to(rows[lo], (n, rows[0].shape[1]))
    for i in range(1, n):
        out = jnp.where(rid == i, rows[lo + i], out)
    return out


def _pair_candidates(s1, s2):
    k = PEER_TOPK
    tm = s1[0].shape[1]
    r = lax.broadcasted_iota(jnp.int32, (8, tm), 0)
    a_lo, a_hi = _stack_rows(s1, 0, 8), _stack_rows(s1, 8, 16)
    b_lo, b_hi = _stack_rows(s2, 0, 8), _stack_rows(s2, 8, 16)

    def slab(vals, tag, lo, hi):
        keep = jnp.where(r >= lo, r, hi + 1) <= hi
        return jnp.where(keep, vals, -jnp.inf), tag.astype(F32)

    slabs = [slab(a_lo + s2[0], r * k, 0, 7),
             slab(a_hi + s2[0], (r + 8) * k, 0, 7),
             slab(s1[0] + b_lo, r, 1, 7),
             slab(s1[0] + b_hi, r + 8, 0, 7),
             slab(a_lo + s2[1], r * k + 1, 1, 7),
             slab(s1[1] + b_lo, r + k, 2, 7),
             slab(a_lo + s2[2], r * k + 2, 2, 4),
             slab(a_lo + s2[3], r * k + 3, 2, 3),
             slab(a_lo + s2[4], r * k + 4, 2, 2)]
    return [v for v, _ in slabs], [t for _, t in slabs]


def _route_kernel(h_ref, g_ref, wq_ref, keys_ref, xn_ref, ids_ref, gates_ref,
                  q_ref, ids_t, gates_t):
    xn = _rms(h_ref[...], g_ref[...])
    xn_ref[...] = xn
    q_ref[...] = jnp.dot(xn.astype(BF16), wq_ref[...], preferred_element_type=F32)
    tm = h_ref.shape[0]
    row8 = lax.broadcasted_iota(jnp.int32, (8, tm), 0)
    key_tags = [(row8 + 8 * g).astype(F32) for g in range(N_KEYS // 8)]
    key_parts = [_split_bf16(keys_ref[p]) for p in range(2)]

    def one_head(hd):
        tops = []
        for p in range(2):
            c0 = pl.multiple_of((hd * 2 + p) * PEER_HALF, PEER_HALF)
            q_hi, q_lo = _split_bf16(q_ref[:, pl.ds(c0, PEER_HALF)])
            k_hi, k_lo = key_parts[p]
            nt = (((1,), (1,)), ((), ()))
            sc = (lax.dot_general(k_hi, q_hi, nt, preferred_element_type=F32)
                  + lax.dot_general(k_hi, q_lo, nt, preferred_element_type=F32)
                  + lax.dot_general(k_lo, q_hi, nt, preferred_element_type=F32))
            slabs = [sc[8 * g:8 * g + 8, :] for g in range(N_KEYS // 8)]
            tops.append(_topk_columns(slabs, key_tags, PEER_TOPK))
        (s1, i1), (s2, i2) = tops
        cand, flat = _pair_candidates(s1, s2)
        best_s, best_c = _topk_columns(cand, flat, PEER_TOPK)
        best_s = _stack_rows(best_s, 0, PEER_TOPK)
        flat_c = _stack_rows(best_c, 0, PEER_TOPK).astype(jnp.int32)
        a_idx = flat_c >> 4
        b_idx = flat_c & (PEER_TOPK - 1)
        e1 = jnp.zeros(flat_c.shape, F32)
        e2 = jnp.zeros(flat_c.shape, F32)
        for a in range(PEER_TOPK):
            e1 = jnp.where(a_idx == a, i1[a], e1)
            e2 = jnp.where(b_idx == a, i2[a], e2)
        ex = jnp.exp(best_s - best_s[0:1, :])
        r0 = pl.multiple_of(hd * PEER_TOPK, PEER_TOPK)
        gates_t[pl.ds(r0, PEER_TOPK), :] = ex / jnp.sum(ex, axis=0, keepdims=True)
        ids_t[pl.ds(r0, PEER_TOPK), :] = ((e1 * N_KEYS + e2) * SLAB).astype(jnp.int32)

    def heads(step, carry):
        for i in range(ROUTE_HEADS_PER_STEP):
            one_head(step * ROUTE_HEADS_PER_STEP + i)
        return carry

    lax.fori_loop(0, PEER_HEADS // ROUTE_HEADS_PER_STEP, heads, 0)
    ids_ref[...] = ids_t[...].T
    gates_ref[...] = gates_t[...].T


def route(h, g, wq, keys, tm=128):
    t = h.shape[0]
    return pl.pallas_call(
        _route_kernel,
        grid=(t // tm,),
        in_specs=[pl.BlockSpec((tm, D_MODEL), lambda i: (i, 0)),
                  pl.BlockSpec((1, D_MODEL), lambda i: (0, 0)),
                  pl.BlockSpec(wq.shape, lambda i: (0, 0)),
                  pl.BlockSpec(keys.shape, lambda i: (0, 0, 0))],
        out_specs=[pl.BlockSpec((tm, D_MODEL), lambda i: (i, 0)),
                   pl.BlockSpec((tm, N_PAIRS), lambda i: (i, 0)),
                   pl.BlockSpec((tm, N_PAIRS), lambda i: (i, 0))],
        out_shape=[jax.ShapeDtypeStruct((t, D_MODEL), F32),
                   jax.ShapeDtypeStruct((t, N_PAIRS), jnp.int32),
                   jax.ShapeDtypeStruct((t, N_PAIRS), F32)],
        scratch_shapes=[pltpu.VMEM((tm, 2 * PEER_HALF * PEER_HEADS), F32),
                        pltpu.VMEM((N_PAIRS, tm), jnp.int32),
                        pltpu.VMEM((N_PAIRS, tm), F32)],
        compiler_params=_params(("arbitrary",), VMEM_LIMIT),
        name="peer_route",
    )(h, g, wq, keys)


def _gather_tokens(ids_ref, tab_ref, tiles, t0):
    zero = lax.min(t0, 0)
    offs = [zero + u for u in range(8)]
    for jj in range(N_PAIRS // 8):
        rows = [ids_ref.at[t0 + q, pl.ds(8 * jj, 8)] for q in range(len(tiles))]
        for u in range(8):
            j = 8 * jj + u
            for q, tile in enumerate(tiles):
                e = pl.multiple_of(rows[q][offs[u]], SLAB)
                tile[pl.ds(j, SLAB, stride=TILE_STRIDE), :] = tab_ref[pl.ds(e, SLAB), :]


def _pipelined_tokens(tb, tiles, consume, ids_ref, tab_ref):
    nfl = len(tiles)

    def body(k, carry):
        t0 = nfl * k
        for q in range(nfl):
            consume(jnp.maximum(t0 - nfl + q, 0), tiles[q])
        _gather_tokens(ids_ref, tab_ref, tiles, t0)
        return carry

    lax.fori_loop(0, tb // nfl, body, 0)
    for q in range(nfl):
        consume(tb - nfl + q, tiles[q])


def _tile_chunk(tile, s):
    return pltpu.bitcast(tile[pl.ds(s * TILE_STRIDE, N_PAIRS), :], BF16)


def _lane_parity(shape):
    lane = lax.broadcasted_iota(jnp.int32, shape, len(shape) - 1)
    return (lane & 1) == 1


def _peer_u_kernel(ids_ref, x_ref, gates_ref, tab_ref, w_ref, sc_ref, *tiles):
    tb = x_ref.shape[0]

    @pl.when(pl.program_id(0) == 0)
    def _():
        for tile in tiles:
            tile[...] = jnp.zeros_like(tile)

    def scores(t, tile):
        g = jnp.concatenate([_tile_chunk(tile, s) for s in range(SLAB)], axis=-1)
        xr = x_ref[pl.ds(t, 1), :]
        xa_hi, xa_lo = _split_bf16(xr[:, :ROW_WORDS])
        xb_hi, xb_lo = _split_bf16(xr[:, ROW_WORDS:])
        x8 = jnp.concatenate([xa_hi, xb_hi, xa_lo, xb_lo,
                              jnp.zeros((4, ROW_WORDS), BF16)], axis=0)
        res = lax.dot_general(x8, g, (((1,), (1,)), ((), ())),
                              preferred_element_type=F32)
        odd = _lane_parity((1, 2 * N_PAIRS))
        part = jnp.where(odd, res[1:2] + res[3:4], res[0:1] + res[2:3])
        sc_ref[pl.ds(t, 1), :] = part + pltpu.roll(part, 1, axis=1)

    _pipelined_tokens(tb, tiles, scores, ids_ref, tab_ref)

    r = lax.broadcasted_iota(jnp.int32, (N_PAIRS, 2 * N_PAIRS), 0)
    c = lax.broadcasted_iota(jnp.int32, (N_PAIRS, 2 * N_PAIRS), 1)
    spread = jnp.where(c == 2 * r + 1, 1.0, 0.0).astype(BF16)
    g_hi, g_lo = _split_bf16(gates_ref[...])
    g2 = (jnp.dot(g_hi, spread, preferred_element_type=F32)
          + jnp.dot(g_lo, spread, preferred_element_type=F32))
    w_ref[...] = g2 * _gelu(sc_ref[...])


def _table_spec():
    return pl.BlockSpec((N_EXPERTS * SLAB, LANES), lambda i: (0, 0),
                        pipeline_mode=pl.Buffered(1))


def _tile_scratch(n):
    return [pltpu.VMEM((SLAB * TILE_STRIDE, LANES), jnp.int32) for _ in range(n)]


def peer_scores(ids, xn, gates, tab, tb=512):
    t = xn.shape[0]
    return pl.pallas_call(
        _peer_u_kernel,
        grid=(t // tb,),
        in_specs=[pl.BlockSpec((tb, N_PAIRS), lambda i: (i, 0), memory_space=pltpu.SMEM),
                  pl.BlockSpec((tb, D_MODEL), lambda i: (i, 0)),
                  pl.BlockSpec((tb, N_PAIRS), lambda i: (i, 0)),
                  _table_spec()],
        out_specs=pl.BlockSpec((tb, 2 * N_PAIRS), lambda i: (i, 0)),
        out_shape=jax.ShapeDtypeStruct((t, 2 * N_PAIRS), F32),
        scratch_shapes=([pltpu.VMEM((tb, 2 * N_PAIRS), F32)]
                        + _tile_scratch(SCORES_TOKENS_IN_FLIGHT)),
        compiler_params=_params(("arbitrary",), VMEM_LIMIT),
        name="peer_u",
    )(ids, xn, gates, tab)


def _peer_v_kernel(ids_ref, w_ref, h_ref, tab_ref, o_ref, acc_ref, lhs_refs, *tiles):
    tb = h_ref.shape[0]

    @pl.when(pl.program_id(0) == 0)
    def _():
        for tile in tiles:
            tile[...] = jnp.zeros_like(tile)

    w_odd = w_ref[...]
    w_even = pltpu.roll(w_odd, 2 * N_PAIRS - 1, axis=1)
    for i, w in enumerate((w_even, w_odd)):
        hi = w.astype(BF16).astype(F32)
        lhs_refs[i, :, :] = hi
        lhs_refs[2 + i, :, :] = w - hi

    def combine(t, tile):
        rid = lax.broadcasted_iota(jnp.int32, (8, 2 * N_PAIRS), 0)
        lhs = jnp.zeros((8, 2 * N_PAIRS), F32)
        for i in range(4):
            row = lhs_refs[i, pl.ds(t, 1), :]
            lhs = jnp.where(rid == i, row, lhs)
        lhs = lhs.astype(BF16)
        lo, hi = [], []
        for s in range(SLAB):
            res = jnp.dot(lhs, _tile_chunk(tile, s), preferred_element_type=F32)
            lo.append(res[0:1] + res[2:3])
            hi.append(res[1:2] + res[3:4])
        acc_ref[pl.ds(t, 1), :] = jnp.concatenate(lo + hi, axis=-1)

    _pipelined_tokens(tb, tiles, combine, ids_ref, tab_ref)
    o_ref[...] = h_ref[...] + acc_ref[...]


def peer_combine(ids, w, h, tab, tb=512):
    t = h.shape[0]
    return pl.pallas_call(
        _peer_v_kernel,
        grid=(t // tb,),
        in_specs=[pl.BlockSpec((tb, N_PAIRS), lambda i: (i, 0), memory_space=pltpu.SMEM),
                  pl.BlockSpec((tb, 2 * N_PAIRS), lambda i: (i, 0)),
                  pl.BlockSpec((tb, D_MODEL), lambda i: (i, 0)),
                  _table_spec()],
        out_specs=pl.BlockSpec((tb, D_MODEL), lambda i: (i, 0)),
        out_shape=jax.ShapeDtypeStruct((t, D_MODEL), F32),
        scratch_shapes=[pltpu.VMEM((tb, D_MODEL), F32),
                        pltpu.VMEM((4, tb, 2 * N_PAIRS), F32)]
                       + _tile_scratch(COMBINE_TOKENS_IN_FLIGHT),
        compiler_params=_params(("arbitrary",), VMEM_LIMIT),
        name="peer_v",
    )(ids, w, h, tab)


def _ple_kernel(h_ref, p_ref, g_ref, wg_ref, wp_ref, gf_ref, o_ref, *, final):
    h = h_ref[...]
    gate = _sigmoid(jnp.dot(_rms(h, g_ref[...]).astype(BF16), wg_ref[...],
                            preferred_element_type=F32))
    out = h + gate * jnp.dot(p_ref[...].astype(BF16), wp_ref[...],
                             preferred_element_type=F32)
    if final:
        out = _rms(out, gf_ref[...])
    o_ref[...] = out


def ple(h, p, layer, g, wg, wp, gf, final, tm=512):
    t = h.shape[0]
    tm = min(tm, t)
    return pl.pallas_call(
        functools.partial(_ple_kernel, final=final),
        grid=(t // tm,),
        in_specs=[pl.BlockSpec((tm, D_MODEL), lambda i: (i, 0)),
                  pl.BlockSpec((None, tm, PLE_DIM), lambda i: (layer, i, 0)),
                  pl.BlockSpec((1, D_MODEL), lambda i: (0, 0)),
                  pl.BlockSpec(wg.shape, lambda i: (0, 0)),
                  pl.BlockSpec(wp.shape, lambda i: (0, 0)),
                  pl.BlockSpec((1, D_MODEL), lambda i: (0, 0))],
        out_specs=pl.BlockSpec((tm, D_MODEL), lambda i: (i, 0)),
        out_shape=jax.ShapeDtypeStruct((t, D_MODEL), F32),
        compiler_params=_params(("arbitrary",), VMEM_LIMIT),
        name="ple",
    )(h, p, g, wg, wp, gf)


def _permute_w_in(w):
    q, k, v, rest = (w[:, :D_MODEL], w[:, D_MODEL:D_MODEL + KV_W],
                     w[:, D_MODEL + KV_W:D_MODEL + 2 * KV_W], w[:, D_MODEL + 2 * KV_W:])
    return jnp.concatenate([q, rest, k, v], axis=1).astype(BF16)


def _pack_table(tab):
    t16 = lax.bitcast_convert_type(tab.astype(BF16), jnp.uint16).astype(jnp.uint32)
    words = t16[:, :ROW_WORDS] | (t16[:, ROW_WORDS:] << 16)
    return lax.bitcast_convert_type(words, jnp.int32).reshape(N_EXPERTS * SLAB, LANES)


def kernel(x, p, norm_mix, w_in, attn_sinks, conv_w, rec_conv_w, rec_conv_b, w_rgate, b_rgate, w_igate, b_igate, lru_lambda, w_branch, w_out, norm_ffn, w_peer_q, peer_sub_keys, peer_u, peer_v, norm_ple, w_ple_gate, w_ple_proj, norm_final):
    batch, seq, d = x.shape
    depth = w_in.shape[0]
    t = batch * seq
    row = lambda a: a.reshape(1, -1)
    h = x.reshape(t, d)
    p3 = p.reshape(depth, t, -1)
    for l in range(depth):
        z = norm_matmul(h, row(norm_mix[l]), _permute_w_in(w_in[l]))
        attn = attention(z, attn_sinks[l], batch, seq)
        conv, rec = conv_rec(z, conv_w[l], rec_conv_w[l], row(rec_conv_b[l]),
                             w_rgate[l].astype(BF16), w_igate[l].astype(BF16),
                             row(b_rgate[l]), row(b_igate[l]), row(lru_lambda[l]), batch, seq)
        h = merge(attn, conv, rec, z, h, w_branch[l].astype(BF16), w_out[l].astype(BF16))
        xn, ids, gates = route(h, row(norm_ffn[l]), w_peer_q[l].astype(BF16), peer_sub_keys[l])
        w = peer_scores(ids, xn, gates, _pack_table(peer_u[l]))
        h = peer_combine(ids, w, h, _pack_table(peer_v[l]))
        h = ple(h, p3, l, row(norm_ple[l]), w_ple_gate[l].astype(BF16),
                w_ple_proj[l].astype(BF16), row(norm_final), final=(l == depth - 1))
    return h.reshape(batch, seq, d)
```

```python
import functools

import jax
import jax.numpy as jnp
from jax import lax
from jax.experimental import pallas as pl
from jax.experimental.pallas import tpu as pltpu

F32 = jnp.float32
BF16 = jnp.bfloat16

D_MODEL = 1024
N_Q_HEADS = 16
N_KV_HEADS = 4
HEAD_DIM = 64
GROUP = N_Q_HEADS // N_KV_HEADS
ATTN_BLOCK = 128
KV_W = N_KV_HEADS * HEAD_DIM
REC_HEADS = 4
REC_HEAD_DIM = D_MODEL // REC_HEADS
LRU_C = 8.0
N_KEYS = 128
N_EXPERTS = N_KEYS * N_KEYS
PEER_HEADS = 8
PEER_TOPK = 16
PEER_HALF = 128
N_PAIRS = PEER_HEADS * PEER_TOPK
PLE_DIM = 256
EPS = 1e-6
NEG_INF = -1e30
TAG_NONE = float(1 << 20)
ROUTE_HEADS_PER_STEP = 8

COL_Q, COL_CB, COL_CC, COL_CX, COL_RX, COL_RY, COL_G0 = 0, 1, 2, 3, 4, 5, 6
IN_COLS = 9 * D_MODEL + 2 * KV_W
COL_K_KV = 9 * D_MODEL // KV_W
COL_V_KV = COL_K_KV + 1

LANES = 128
SLAB = D_MODEL // 2 // LANES
ROW_WORDS = SLAB * LANES
TILE_STRIDE = N_PAIRS + 8
INDEX_COLS = 8
SCORES_TOKENS_IN_FLIGHT = 16
COMBINE_TOKENS_IN_FLIGHT = 16
VMEM_LIMIT = 56 * 1024 * 1024


def _params(sem, vmem=None):
    return pltpu.CompilerParams(dimension_semantics=sem, vmem_limit_bytes=vmem)


def _rms(x32, g):
    ms = jnp.mean(x32 * x32, axis=-1, keepdims=True)
    return x32 * lax.rsqrt(ms + EPS) * g


def _sigmoid(x):
    return 0.5 * jnp.tanh(0.5 * x) + 0.5


def _gelu(x):
    return 0.5 * x * (1.0 + jnp.tanh(0.7978845608028654 * (x + 0.044715 * (x * x * x))))


def _split_bf16(x32):
    hi = x32.astype(BF16)
    lo = (x32 - hi.astype(F32)).astype(BF16)
    return hi, lo


def _norm_matmul_kernel(x_ref, g_ref, w_ref, o_ref, xn_ref):
    @pl.when(pl.program_id(1) == 0)
    def _():
        xn_ref[...] = _rms(x_ref[...], g_ref[...]).astype(BF16)

    o_ref[...] = jnp.dot(xn_ref[...], w_ref[...],
                         preferred_element_type=F32).astype(o_ref.dtype)


def norm_matmul(x, g, w, tm=2048, tn=1024):
    m, d = x.shape
    n = w.shape[1]
    tm = min(tm, m)
    return pl.pallas_call(
        _norm_matmul_kernel,
        grid=(m // tm, pl.cdiv(n, tn)),
        in_specs=[pl.BlockSpec((tm, d), lambda i, j: (i, 0)),
                  pl.BlockSpec((1, d), lambda i, j: (0, 0)),
                  pl.BlockSpec((d, tn), lambda i, j: (0, j))],
        out_specs=pl.BlockSpec((tm, tn), lambda i, j: (i, j)),
        out_shape=jax.ShapeDtypeStruct((m, n), BF16),
        scratch_shapes=[pltpu.VMEM((tm, d), BF16)],
        compiler_params=_params(("arbitrary", "arbitrary"), VMEM_LIMIT),
        name="norm_matmul",
    )(x, g, w)


def _attn_kernel(sink_ref, q_ref, kc_ref, vc_ref, kp_ref, vp_ref, o_ref):
    n = pl.program_id(1)
    q = q_ref[...] * (HEAD_DIM ** -0.5)
    k = jnp.concatenate([kp_ref[...], kc_ref[...]], axis=0)
    v = jnp.concatenate([vp_ref[...], vc_ref[...]], axis=0)
    shape = (2 * ATTN_BLOCK, GROUP * ATTN_BLOCK)
    ki = lax.broadcasted_iota(jnp.int32, shape, 0)
    col = lax.broadcasted_iota(jnp.int32, shape, 1)
    diff = (col & (ATTN_BLOCK - 1)) + ATTN_BLOCK - ki
    first = jnp.where(n > 0, 0, ATTN_BLOCK)
    inside = jnp.where(diff >= 0, jnp.where(diff < ATTN_BLOCK, ki - first, -1), -1)
    mask = inside >= 0
    grp = lax.broadcasted_iota(jnp.int32, (1, GROUP * ATTN_BLOCK), 1) // ATTN_BLOCK
    ones = jnp.ones((2 * ATTN_BLOCK, HEAD_DIM), BF16)
    scores, sinks = [], []
    for h in range(N_KV_HEADS):
        kh = k[:, h * HEAD_DIM:(h + 1) * HEAD_DIM]
        qh = jnp.concatenate([q[:, (h * GROUP + g) * HEAD_DIM:(h * GROUP + g + 1) * HEAD_DIM]
                              for g in range(GROUP)], axis=0)
        sink = jnp.zeros((1, GROUP * ATTN_BLOCK), F32)
        for g in range(GROUP):
            sink = jnp.where(grp == g, sink_ref[h * GROUP + g], sink)
        sinks.append(sink)
        scores.append(lax.dot_general(kh, qh, (((1,), (1,)), ((), ())),
                                      preferred_element_type=F32))
    probs, sink_terms = [], []
    for h in range(N_KV_HEADS):
        s = jnp.where(mask, scores[h], NEG_INF)
        m = jnp.maximum(jnp.max(s, axis=0, keepdims=True), sinks[h])
        probs.append(jnp.exp(s - m).astype(BF16))
        sink_terms.append(jnp.exp(sinks[h] - m))
    outs = []
    for h in range(N_KV_HEADS):
        vh = jnp.concatenate([v[:, h * HEAD_DIM:(h + 1) * HEAD_DIM], ones], axis=-1)
        ov = lax.dot_general(vh, probs[h], (((0,), (0,)), ((), ())),
                             preferred_element_type=F32)
        denom = ov[HEAD_DIM:HEAD_DIM + 1, :] + sink_terms[h]
        o = (ov[:HEAD_DIM, :] / denom).T
        outs.extend(o[g * ATTN_BLOCK:(g + 1) * ATTN_BLOCK, :] for g in range(GROUP))
    o_ref[...] = jnp.concatenate(outs, axis=-1).astype(o_ref.dtype)


def attention(z, sinks, batch, seq):
    nb = seq // ATTN_BLOCK
    t = batch * seq
    cur = lambda b, n: b * nb + n
    prev = lambda b, n: b * nb + jnp.maximum(n - 1, 0)
    return pl.pallas_call(
        _attn_kernel,
        grid=(batch, nb),
        in_specs=[pl.BlockSpec(memory_space=pltpu.SMEM),
                  pl.BlockSpec((ATTN_BLOCK, D_MODEL), lambda b, n: (cur(b, n), COL_Q)),
                  pl.BlockSpec((ATTN_BLOCK, KV_W), lambda b, n: (cur(b, n), COL_K_KV)),
                  pl.BlockSpec((ATTN_BLOCK, KV_W), lambda b, n: (cur(b, n), COL_V_KV)),
                  pl.BlockSpec((ATTN_BLOCK, KV_W), lambda b, n: (prev(b, n), COL_K_KV)),
                  pl.BlockSpec((ATTN_BLOCK, KV_W), lambda b, n: (prev(b, n), COL_V_KV))],
        out_specs=pl.BlockSpec((ATTN_BLOCK, D_MODEL), lambda b, n: (cur(b, n), 0)),
        out_shape=jax.ShapeDtypeStruct((t, D_MODEL), BF16),
        compiler_params=_params(("arbitrary", "arbitrary")),
        name="swa_attention",
    )(sinks, z, z, z, z, z)


HALO = 8


def _convrec_kernel(cb_ref, cc_ref, cx_ref, rx_ref, ry_ref, cw_ref, rcw_ref, rcb_ref,
                    wr_ref, wi_ref, br_ref, bi_ref, lam_ref, conv_ref, rec_ref,
                    ubuf, xbuf, hcar):
    ts = cb_ref.shape[0]

    @pl.when(pl.program_id(1) == 0)
    def _():
        ubuf[0:HALO, :] = jnp.zeros((HALO, D_MODEL), F32)
        xbuf[0:HALO, :] = jnp.zeros((HALO, D_MODEL), F32)
        hcar[...] = jnp.zeros_like(hcar)

    u = cc_ref[...].astype(F32) * cx_ref[...].astype(F32)
    ubuf[HALO:HALO + ts, :] = u
    y = (cw_ref[2:3, :] * u + cw_ref[1:2, :] * ubuf[HALO - 1:HALO - 1 + ts, :]
         + cw_ref[0:1, :] * ubuf[HALO - 2:HALO - 2 + ts, :])
    conv_ref[...] = (cb_ref[...].astype(F32) * y).astype(conv_ref.dtype)
    ubuf[0:HALO, :] = ubuf[ts:ts + HALO, :]

    x = rx_ref[...].astype(F32)
    xbuf[HALO:HALO + ts, :] = x
    c = (rcw_ref[3:4, :] * x + rcw_ref[2:3, :] * xbuf[HALO - 1:HALO - 1 + ts, :]
         + rcw_ref[1:2, :] * xbuf[HALO - 2:HALO - 2 + ts, :]
         + rcw_ref[0:1, :] * xbuf[HALO - 3:HALO - 3 + ts, :] + rcb_ref[...])
    xbuf[0:HALO, :] = xbuf[ts:ts + HALO, :]
    cbf = c.astype(BF16)
    r_parts, i_parts = [], []
    for h in range(REC_HEADS):
        ch = cbf[:, h * REC_HEAD_DIM:(h + 1) * REC_HEAD_DIM]
        r_parts.append(jnp.dot(ch, wr_ref[h], preferred_element_type=F32))
        i_parts.append(jnp.dot(ch, wi_ref[h], preferred_element_type=F32))
    r = _sigmoid(jnp.concatenate(r_parts, axis=-1) + br_ref[...])
    ig = _sigmoid(jnp.concatenate(i_parts, axis=-1) + bi_ref[...])
    nl = -lam_ref[...]
    softplus = jnp.maximum(nl, 0.0) + jnp.log(1.0 + jnp.exp(-jnp.abs(nl)))
    log_a = -LRU_C * r * softplus
    a = jnp.exp(log_a)
    b = jnp.sqrt(1.0 - jnp.exp(2.0 * log_a)) * (ig * c)
    a3 = a.reshape(ts // 8, 8, D_MODEL)
    b3 = b.reshape(ts // 8, 8, D_MODEL)
    rid = lax.broadcasted_iota(jnp.int32, a3.shape, 1)
    for d in (1, 2, 4):
        inside = rid >= d
        a_sh = jnp.where(inside, pltpu.roll(a3, d, axis=1), 1.0)
        b_sh = jnp.where(inside, pltpu.roll(b3, d, axis=1), 0.0)
        b3 = a3 * b_sh + b3
        a3 = a3 * a_sh
    carry = hcar[0:1, :]
    groups = []
    for g in range(ts // 8):
        hg = b3[g] + a3[g] * carry
        groups.append(hg)
        carry = hg[7:8, :]
    hcar[0:1, :] = carry
    hs = jnp.concatenate(groups, axis=0)
    rec_ref[...] = (_gelu(ry_ref[...].astype(F32)) * hs).astype(rec_ref.dtype)


def conv_rec(z, cw, rcw, rcb, wr, wi, br, bi, lam, batch, seq, ts=256):
    ts = min(ts, seq)
    nt = seq // ts
    t = batch * seq
    row = lambda b, i: b * nt + i
    zspec = lambda col: pl.BlockSpec((ts, D_MODEL), lambda b, i: (row(b, i), col))
    full = lambda a: pl.BlockSpec(a.shape, lambda b, i: (0,) * a.ndim)
    return pl.pallas_call(
        _convrec_kernel,
        grid=(batch, nt),
        in_specs=[zspec(COL_CB), zspec(COL_CC), zspec(COL_CX), zspec(COL_RX), zspec(COL_RY),
                  full(cw), full(rcw), full(rcb), full(wr), full(wi), full(br), full(bi),
                  full(lam)],
        out_specs=[pl.BlockSpec((ts, D_MODEL), lambda b, i: (row(b, i), 0)),
                   pl.BlockSpec((ts, D_MODEL), lambda b, i: (row(b, i), 0))],
        out_shape=[jax.ShapeDtypeStruct((t, D_MODEL), BF16),
                   jax.ShapeDtypeStruct((t, D_MODEL), BF16)],
        scratch_shapes=[pltpu.VMEM((ts + HALO, D_MODEL), F32),
                        pltpu.VMEM((ts + HALO, D_MODEL), F32),
                        pltpu.VMEM((HALO, D_MODEL), F32)],
        compiler_params=_params(("arbitrary", "arbitrary"), VMEM_LIMIT),
        name="conv_rec",
    )(z, z, z, z, z, cw, rcw, rcb, wr, wi, br, bi, lam)


def _merge_kernel(a_ref, c_ref, r_ref, g0_ref, g1_ref, g2_ref, h_ref, wb_ref, wo_ref, o_ref):
    merged = None
    for n, (br, gt) in enumerate(((a_ref, g0_ref), (c_ref, g1_ref), (r_ref, g2_ref))):
        y = jnp.dot(br[...], wb_ref[n], preferred_element_type=F32)
        term = _sigmoid(gt[...].astype(F32)) * y
        merged = term if merged is None else merged + term
    o_ref[...] = h_ref[...] + jnp.dot(merged.astype(BF16), wo_ref[...],
                                      preferred_element_type=F32)


def merge(attn, conv, rec, z, h, wb, wo, tm=512):
    t = h.shape[0]
    tm = min(tm, t)
    blk = lambda col: pl.BlockSpec((tm, D_MODEL), lambda i: (i, col))
    return pl.pallas_call(
        _merge_kernel,
        grid=(t // tm,),
        in_specs=[blk(0), blk(0), blk(0), blk(COL_G0), blk(COL_G0 + 1), blk(COL_G0 + 2), blk(0),
                  pl.BlockSpec(wb.shape, lambda i: (0, 0, 0)),
                  pl.BlockSpec(wo.shape, lambda i: (0, 0))],
        out_specs=blk(0),
        out_shape=jax.ShapeDtypeStruct((t, D_MODEL), F32),
        compiler_params=_params(("arbitrary",), VMEM_LIMIT),
        name="merge",
    )(attn, conv, rec, z, z, z, h, wb, wo)


def _sorting_network(n):
    pairs = []
    p = 1
    while p < n:
        k = p
        while k >= 1:
            for j in range(k % p, n - k, 2 * k):
                for i in range(min(k, n - j - k)):
                    if (i + j) // (2 * p) == (i + j + k) // (2 * p):
                        pairs.append((i + j, i + j + k))
            k //= 2
        p *= 2
    return pairs


def _topk_columns(vals, tags, k):
    vals, tags = list(vals), list(tags)
    n = len(vals)
    for i, j in _sorting_network(n):
        va, ta, vb, tb = vals[i], tags[i], vals[j], tags[j]
        first = (va > vb) | ((va == vb) & (ta < tb))
        vals[i], vals[j] = jnp.where(first, va, vb), jnp.where(first, vb, va)
        tags[i], tags[j] = jnp.where(first, ta, tb), jnp.where(first, tb, ta)
    out_v, out_t = [], []
    for it in range(k):
        m = jnp.max(vals[0], axis=0, keepdims=True)
        best = jnp.min(jnp.where(vals[0] == m, tags[0], TAG_NONE), axis=0, keepdims=True)
        out_v.append(m)
        out_t.append(best)
        left = k - 1 - it
        pop = tags[0] == best
        for d in range(min(left, n - 1)):
            vals[d] = jnp.where(pop, vals[d + 1], vals[d])
            tags[d] = jnp.where(pop, tags[d + 1], tags[d])
        if left > n - 1:
            vals[n - 1] = jnp.where(pop, -jnp.inf, vals[n - 1])
            tags[n - 1] = jnp.where(pop, TAG_NONE, tags[n - 1])
    return out_v, out_t


def _stack_rows(rows, lo, hi):
    n = hi - lo
    rid = lax.broadcasted_iota(jnp.int32, (n, rows[0].shape[1]), 0)
    out = jnp.broadcast_to(rows[lo], (n, rows[0].shape[1]))
    for i in range(1, n):
        out = jnp.where(rid == i, rows[lo + i], out)
    return out


def _pair_candidates(s1, s2):
    k = PEER_TOPK
    tm = s1[0].shape[1]
    r = lax.broadcasted_iota(jnp.int32, (8, tm), 0)
    a_lo, a_hi = _stack_rows(s1, 0, 8), _stack_rows(s1, 8, 16)
    b_lo, b_hi = _stack_rows(s2, 0, 8), _stack_rows(s2, 8, 16)

    def slab(vals, tag, lo, hi):
        keep = jnp.where(r >= lo, r, hi + 1) <= hi
        return jnp.where(keep, vals, -jnp.inf), tag.astype(F32)

    slabs = [slab(a_lo + s2[0], r * k, 0, 7),
             slab(a_hi + s2[0], (r + 8) * k, 0, 7),
             slab(s1[0] + b_lo, r, 1, 7),
             slab(s1[0] + b_hi, r + 8, 0, 7),
             slab(a_lo + s2[1], r * k + 1, 1, 7),
             slab(s1[1] + b_lo, r + k, 2, 7),
             slab(a_lo + s2[2], r * k + 2, 2, 4),
             slab(a_lo + s2[3], r * k + 3, 2, 3),
             slab(a_lo + s2[4], r * k + 4, 2, 2)]
    return [v for v, _ in slabs], [t for _, t in slabs]


def _route_kernel(h_ref, g_ref, wq_ref, keys_ref, xn_ref, ids_ref, gates_ref,
                  q_ref, ids_t, gates_t):
    xn = _rms(h_ref[...], g_ref[...])
    xn_ref[...] = xn
    q_ref[...] = jnp.dot(xn.astype(BF16), wq_ref[...], preferred_element_type=F32)
    tm = h_ref.shape[0]
    row8 = lax.broadcasted_iota(jnp.int32, (8, tm), 0)
    key_tags = [(row8 + 8 * g).astype(F32) for g in range(N_KEYS // 8)]
    key_parts = [_split_bf16(keys_ref[p]) for p in range(2)]

    def one_head(hd):
        tops = []
        for p in range(2):
            c0 = pl.multiple_of((hd * 2 + p) * PEER_HALF, PEER_HALF)
            q_hi, q_lo = _split_bf16(q_ref[:, pl.ds(c0, PEER_HALF)])
            k_hi, k_lo = key_parts[p]
            nt = (((1,), (1,)), ((), ()))
            sc = (lax.dot_general(k_hi, q_hi, nt, preferred_element_type=F32)
                  + lax.dot_general(k_hi, q_lo, nt, preferred_element_type=F32)
                  + lax.dot_general(k_lo, q_hi, nt, preferred_element_type=F32))
            slabs = [sc[8 * g:8 * g + 8, :] for g in range(N_KEYS // 8)]
            tops.append(_topk_columns(slabs, key_tags, PEER_TOPK))
        (s1, i1), (s2, i2) = tops
        cand, flat = _pair_candidates(s1, s2)
        best_s, best_c = _topk_columns(cand, flat, PEER_TOPK)
        best_s = _stack_rows(best_s, 0, PEER_TOPK)
        flat_c = _stack_rows(best_c, 0, PEER_TOPK).astype(jnp.int32)
        a_idx = flat_c >> 4
        b_idx = flat_c & (PEER_TOPK - 1)
        e1 = jnp.zeros(flat_c.shape, F32)
        e2 = jnp.zeros(flat_c.shape, F32)
        for a in range(PEER_TOPK):
            e1 = jnp.where(a_idx == a, i1[a], e1)
            e2 = jnp.where(b_idx == a, i2[a], e2)
        ex = jnp.exp(best_s - best_s[0:1, :])
        r0 = pl.multiple_of(hd * PEER_TOPK, PEER_TOPK)
        gates_t[pl.ds(r0, PEER_TOPK), :] = ex / jnp.sum(ex, axis=0, keepdims=True)
        ids_t[pl.ds(r0, PEER_TOPK), :] = ((e1 * N_KEYS + e2) * SLAB).astype(jnp.int32)

    def heads(step, carry):
        for i in range(ROUTE_HEADS_PER_STEP):
            one_head(step * ROUTE_HEADS_PER_STEP + i)
        return carry

    lax.fori_loop(0, PEER_HEADS // ROUTE_HEADS_PER_STEP, heads, 0)
    ids_ref[...] = ids_t[...].T
    gates_ref[...] = gates_t[...].T


def route(h, g, wq, keys, tm=128):
    t = h.shape[0]
    return pl.pallas_call(
        _route_kernel,
        grid=(t // tm,),
        in_specs=[pl.BlockSpec((tm, D_MODEL), lambda i: (i, 0)),
                  pl.BlockSpec((1, D_MODEL), lambda i: (0, 0)),
                  pl.BlockSpec(wq.shape, lambda i: (0, 0)),
                  pl.BlockSpec(keys.shape, lambda i: (0, 0, 0))],
        out_specs=[pl.BlockSpec((tm, D_MODEL), lambda i: (i, 0)),
                   pl.BlockSpec((tm, N_PAIRS), lambda i: (i, 0)),
                   pl.BlockSpec((tm, N_PAIRS), lambda i: (i, 0))],
        out_shape=[jax.ShapeDtypeStruct((t, D_MODEL), F32),
                   jax.ShapeDtypeStruct((t, N_PAIRS), jnp.int32),
                   jax.ShapeDtypeStruct((t, N_PAIRS), F32)],
        scratch_shapes=[pltpu.VMEM((tm, 2 * PEER_HALF * PEER_HEADS), F32),
                        pltpu.VMEM((N_PAIRS, tm), jnp.int32),
                        pltpu.VMEM((N_PAIRS, tm), F32)],
        compiler_params=_params(("arbitrary",), VMEM_LIMIT),
        name="peer_route",
    )(h, g, wq, keys)


def _gather_tokens(ids_ref, tab_ref, tiles, t0):
    zero = lax.min(t0, 0)
    offs = [zero + u for u in range(INDEX_COLS)]
    for jj in range(N_PAIRS // INDEX_COLS):
        rows = [ids_ref.at[t0 + q, pl.ds(INDEX_COLS * jj, INDEX_COLS)] for q in range(len(tiles))]
        for u in range(INDEX_COLS):
            j = INDEX_COLS * jj + u
            for q, tile in enumerate(tiles):
                e = pl.multiple_of(rows[q][offs[u]], SLAB)
                tile[pl.ds(j, SLAB, stride=TILE_STRIDE), :] = tab_ref[pl.ds(e, SLAB), :]


def _pipelined_tokens(tb, tiles, consume, ids_ref, tab_ref):
    nfl = len(tiles)

    def body(k, carry):
        t0 = nfl * k
        for q in range(nfl):
            consume(jnp.maximum(t0 - nfl + q, 0), tiles[q])
        _gather_tokens(ids_ref, tab_ref, tiles, t0)
        return carry

    lax.fori_loop(0, tb // nfl, body, 0)
    for q in range(nfl):
        consume(tb - nfl + q, tiles[q])


def _tile_chunk(tile, s):
    return pltpu.bitcast(tile[pl.ds(s * TILE_STRIDE, N_PAIRS), :], BF16)


def _lane_parity(shape):
    lane = lax.broadcasted_iota(jnp.int32, shape, len(shape) - 1)
    return (lane & 1) == 1


def _peer_u_kernel(ids_ref, x_ref, gates_ref, tab_ref, w_ref, sc_ref, *tiles):
    tb = x_ref.shape[0]

    @pl.when(pl.program_id(0) == 0)
    def _():
        for tile in tiles:
            tile[...] = jnp.zeros_like(tile)

    def scores(t, tile):
        g = jnp.concatenate([_tile_chunk(tile, s) for s in range(SLAB)], axis=-1)
        xr = x_ref[pl.ds(t, 1), :]
        xa_hi, xa_lo = _split_bf16(xr[:, :ROW_WORDS])
        xb_hi, xb_lo = _split_bf16(xr[:, ROW_WORDS:])
        x8 = jnp.concatenate([xa_hi, xb_hi, xa_lo, xb_lo,
                              jnp.zeros((4, ROW_WORDS), BF16)], axis=0)
        res = lax.dot_general(x8, g, (((1,), (1,)), ((), ())),
                              preferred_element_type=F32)
        odd = _lane_parity((1, 2 * N_PAIRS))
        part = jnp.where(odd, res[1:2] + res[3:4], res[0:1] + res[2:3])
        sc_ref[pl.ds(t, 1), :] = part + pltpu.roll(part, 1, axis=1)

    _pipelined_tokens(tb, tiles, scores, ids_ref, tab_ref)

    r = lax.broadcasted_iota(jnp.int32, (N_PAIRS, 2 * N_PAIRS), 0)
    c = lax.broadcasted_iota(jnp.int32, (N_PAIRS, 2 * N_PAIRS), 1)
    spread = jnp.where(c == 2 * r + 1, 1.0, 0.0).astype(BF16)
    g_hi, g_lo = _split_bf16(gates_ref[...])
    g2 = (jnp.dot(g_hi, spread, preferred_element_type=F32)
          + jnp.dot(g_lo, spread, preferred_element_type=F32))
    w_ref[...] = g2 * _gelu(sc_ref[...])


def _table_spec():
    return pl.BlockSpec((N_EXPERTS * SLAB, LANES), lambda i: (0, 0),
                        pipeline_mode=pl.Buffered(1))


def _tile_scratch(n):
    return [pltpu.VMEM((SLAB * TILE_STRIDE, LANES), jnp.int32) for _ in range(n)]


def peer_scores(ids, xn, gates, tab, tb=512):
    t = xn.shape[0]
    return pl.pallas_call(
        _peer_u_kernel,
        grid=(t // tb,),
        in_specs=[pl.BlockSpec((tb, N_PAIRS), lambda i: (i, 0), memory_space=pltpu.SMEM),
                  pl.BlockSpec((tb, D_MODEL), lambda i: (i, 0)),
                  pl.BlockSpec((tb, N_PAIRS), lambda i: (i, 0)),
                  _table_spec()],
        out_specs=pl.BlockSpec((tb, 2 * N_PAIRS), lambda i: (i, 0)),
        out_shape=jax.ShapeDtypeStruct((t, 2 * N_PAIRS), F32),
        scratch_shapes=([pltpu.VMEM((tb, 2 * N_PAIRS), F32)]
                        + _tile_scratch(SCORES_TOKENS_IN_FLIGHT)),
        compiler_params=_params(("arbitrary",), VMEM_LIMIT),
        name="peer_u",
    )(ids, xn, gates, tab)


def _peer_v_kernel(ids_ref, w_ref, h_ref, tab_ref, o_ref, acc_ref, lhs_refs, *tiles):
    tb = h_ref.shape[0]

    @pl.when(pl.program_id(0) == 0)
    def _():
        for tile in tiles:
            tile[...] = jnp.zeros_like(tile)

    w_odd = w_ref[...]
    w_even = pltpu.roll(w_odd, 2 * N_PAIRS - 1, axis=1)
    for i, w in enumerate((w_even, w_odd)):
        hi = w.astype(BF16).astype(F32)
        lhs_refs[i, :, :] = hi
        lhs_refs[2 + i, :, :] = w - hi

    def combine(t, tile):
        rid = lax.broadcasted_iota(jnp.int32, (8, 2 * N_PAIRS), 0)
        lhs = jnp.zeros((8, 2 * N_PAIRS), F32)
        for i in range(4):
            row = lhs_refs[i, pl.ds(t, 1), :]
            lhs = jnp.where(rid == i, row, lhs)
        lhs = lhs.astype(BF16)
        lo, hi = [], []
        for s in range(SLAB):
            res = jnp.dot(lhs, _tile_chunk(tile, s), preferred_element_type=F32)
            lo.append(res[0:1] + res[2:3])
            hi.append(res[1:2] + res[3:4])
        acc_ref[pl.ds(t, 1), :] = jnp.concatenate(lo + hi, axis=-1)

    _pipelined_tokens(tb, tiles, combine, ids_ref, tab_ref)
    o_ref[...] = h_ref[...] + acc_ref[...]


def peer_combine(ids, w, h, tab, tb=512):
    t = h.shape[0]
    return pl.pallas_call(
        _peer_v_kernel,
        grid=(t // tb,),
        in_specs=[pl.BlockSpec((tb, N_PAIRS), lambda i: (i, 0), memory_space=pltpu.SMEM),
                  pl.BlockSpec((tb, 2 * N_PAIRS), lambda i: (i, 0)),
                  pl.BlockSpec((tb, D_MODEL), lambda i: (i, 0)),
                  _table_spec()],
        out_specs=pl.BlockSpec((tb, D_MODEL), lambda i: (i, 0)),
        out_shape=jax.ShapeDtypeStruct((t, D_MODEL), F32),
        scratch_shapes=[pltpu.VMEM((tb, D_MODEL), F32),
                        pltpu.VMEM((4, tb, 2 * N_PAIRS), F32)]
                       + _tile_scratch(COMBINE_TOKENS_IN_FLIGHT),
        compiler_params=_params(("arbitrary",), VMEM_LIMIT),
        name="peer_v",
    )(ids, w, h, tab)


def _ple_kernel(h_ref, p_ref, g_ref, wg_ref, wp_ref, gf_ref, o_ref, *, final):
    h = h_ref[...]
    gate = _sigmoid(jnp.dot(_rms(h, g_ref[...]).astype(BF16), wg_ref[...],
                            preferred_element_type=F32))
    out = h + gate * jnp.dot(p_ref[...].astype(BF16), wp_ref[...],
                             preferred_element_type=F32)
    if final:
        out = _rms(out, gf_ref[...])
    o_ref[...] = out


def ple(h, p, layer, g, wg, wp, gf, final, tm=512):
    t = h.shape[0]
    tm = min(tm, t)
    return pl.pallas_call(
        functools.partial(_ple_kernel, final=final),
        grid=(t // tm,),
        in_specs=[pl.BlockSpec((tm, D_MODEL), lambda i: (i, 0)),
                  pl.BlockSpec((None, tm, PLE_DIM), lambda i: (layer, i, 0)),
                  pl.BlockSpec((1, D_MODEL), lambda i: (0, 0)),
                  pl.BlockSpec(wg.shape, lambda i: (0, 0)),
                  pl.BlockSpec(wp.shape, lambda i: (0, 0)),
                  pl.BlockSpec((1, D_MODEL), lambda i: (0, 0))],
        out_specs=pl.BlockSpec((tm, D_MODEL), lambda i: (i, 0)),
        out_shape=jax.ShapeDtypeStruct((t, D_MODEL), F32),
        compiler_params=_params(("arbitrary",), VMEM_LIMIT),
        name="ple",
    )(h, p, g, wg, wp, gf)


def _permute_w_in(w):
    q, k, v, rest = (w[:, :D_MODEL], w[:, D_MODEL:D_MODEL + KV_W],
                     w[:, D_MODEL + KV_W:D_MODEL + 2 * KV_W], w[:, D_MODEL + 2 * KV_W:])
    return jnp.concatenate([q, rest, k, v], axis=1).astype(BF16)


def _pack_table(tab):
    t16 = lax.bitcast_convert_type(tab.astype(BF16), jnp.uint16).astype(jnp.uint32)
    words = t16[:, :ROW_WORDS] | (t16[:, ROW_WORDS:] << 16)
    return lax.bitcast_convert_type(words, jnp.int32).reshape(N_EXPERTS * SLAB, LANES)


def kernel(x, p, norm_mix, w_in, attn_sinks, conv_w, rec_conv_w, rec_conv_b, w_rgate, b_rgate, w_igate, b_igate, lru_lambda, w_branch, w_out, norm_ffn, w_peer_q, peer_sub_keys, peer_u, peer_v, norm_ple, w_ple_gate, w_ple_proj, norm_final):
    batch, seq, d = x.shape
    depth = w_in.shape[0]
    t = batch * seq
    row = lambda a: a.reshape(1, -1)
    h = x.reshape(t, d)
    p3 = p.reshape(depth, t, -1)
    for l in range(depth):
        z = norm_matmul(h, row(norm_mix[l]), _permute_w_in(w_in[l]))
        attn = attention(z, attn_sinks[l], batch, seq)
        conv, rec = conv_rec(z, conv_w[l], rec_conv_w[l], row(rec_conv_b[l]),
                             w_rgate[l].astype(BF16), w_igate[l].astype(BF16),
                             row(b_rgate[l]), row(b_igate[l]), row(lru_lambda[l]), batch, seq)
        h = merge(attn, conv, rec, z, h, w_branch[l].astype(BF16), w_out[l].astype(BF16))
        xn, ids, gates = route(h, row(norm_ffn[l]), w_peer_q[l].astype(BF16), peer_sub_keys[l])
        w = peer_scores(ids, xn, gates, _pack_table(peer_u[l]))
        h = peer_combine(ids, w, h, _pack_table(peer_v[l]))
        h = ple(h, p3, l, row(norm_ple[l]), w_ple_gate[l].astype(BF16),
                w_ple_proj[l].astype(BF16), row(norm_final), final=(l == depth - 1))
    return h.reshape(batch, seq, d)
```

```python
import functools

import jax
import jax.numpy as jnp
from jax import lax
from jax.experimental import pallas as pl
from jax.experimental.pallas import tpu as pltpu

F32 = jnp.float32
BF16 = jnp.bfloat16

D_MODEL = 1024
N_Q_HEADS = 16
N_KV_HEADS = 4
HEAD_DIM = 64
GROUP = N_Q_HEADS // N_KV_HEADS
ATTN_BLOCK = 128
KV_W = N_KV_HEADS * HEAD_DIM
REC_HEADS = 4
REC_HEAD_DIM = D_MODEL // REC_HEADS
LRU_C = 8.0
N_KEYS = 128
N_EXPERTS = N_KEYS * N_KEYS
PEER_HEADS = 8
PEER_TOPK = 16
PEER_HALF = 128
N_PAIRS = PEER_HEADS * PEER_TOPK
PLE_DIM = 256
EPS = 1e-6
NEG_INF = -1e30
TAG_NONE = float(1 << 20)
ROUTE_HEADS_PER_STEP = 8

COL_Q, COL_CB, COL_CC, COL_CX, COL_RX, COL_RY, COL_G0 = 0, 1, 2, 3, 4, 5, 6
IN_COLS = 9 * D_MODEL + 2 * KV_W
COL_K_KV = 9 * D_MODEL // KV_W
COL_V_KV = COL_K_KV + 1

LANES = 128
SLAB = D_MODEL // 2 // LANES
ROW_WORDS = SLAB * LANES
TILE_STRIDE = N_PAIRS + 8
INDEX_COLS = 8
SCORES_TOKENS_IN_FLIGHT = 16
COMBINE_TOKENS_IN_FLIGHT = 16
VMEM_LIMIT = 56 * 1024 * 1024


def _params(sem, vmem=None):
    return pltpu.CompilerParams(dimension_semantics=sem, vmem_limit_bytes=vmem)


def _rms(x32, g):
    ms = jnp.mean(x32 * x32, axis=-1, keepdims=True)
    return x32 * lax.rsqrt(ms + EPS) * g


def _sigmoid(x):
    return 0.5 * jnp.tanh(0.5 * x) + 0.5


def _gelu(x):
    return 0.5 * x * (1.0 + jnp.tanh(0.7978845608028654 * (x + 0.044715 * (x * x * x))))


def _split_bf16(x32):
    hi = x32.astype(BF16)
    lo = (x32 - hi.astype(F32)).astype(BF16)
    return hi, lo


def _norm_matmul_kernel(x_ref, g_ref, w_ref, o_ref, xn_ref):
    @pl.when(pl.program_id(1) == 0)
    def _():
        xn_ref[...] = _rms(x_ref[...], g_ref[...]).astype(BF16)

    o_ref[...] = jnp.dot(xn_ref[...], w_ref[...],
                         preferred_element_type=F32).astype(o_ref.dtype)


def norm_matmul(x, g, w, tm=2048, tn=1024):
    m, d = x.shape
    n = w.shape[1]
    tm = min(tm, m)
    return pl.pallas_call(
        _norm_matmul_kernel,
        grid=(m // tm, pl.cdiv(n, tn)),
        in_specs=[pl.BlockSpec((tm, d), lambda i, j: (i, 0)),
                  pl.BlockSpec((1, d), lambda i, j: (0, 0)),
                  pl.BlockSpec((d, tn), lambda i, j: (0, j))],
        out_specs=pl.BlockSpec((tm, tn), lambda i, j: (i, j)),
        out_shape=jax.ShapeDtypeStruct((m, n), BF16),
        scratch_shapes=[pltpu.VMEM((tm, d), BF16)],
        compiler_params=_params(("arbitrary", "arbitrary"), VMEM_LIMIT),
        name="norm_matmul",
    )(x, g, w)


def _attn_kernel(sink_ref, q_ref, kc_ref, vc_ref, kp_ref, vp_ref, o_ref):
    n = pl.program_id(1)
    q = q_ref[...] * (HEAD_DIM ** -0.5)
    k = jnp.concatenate([kp_ref[...], kc_ref[...]], axis=0)
    v = jnp.concatenate([vp_ref[...], vc_ref[...]], axis=0)
    shape = (2 * ATTN_BLOCK, GROUP * ATTN_BLOCK)
    ki = lax.broadcasted_iota(jnp.int32, shape, 0)
    col = lax.broadcasted_iota(jnp.int32, shape, 1)
    diff = (col & (ATTN_BLOCK - 1)) + ATTN_BLOCK - ki
    first = jnp.where(n > 0, 0, ATTN_BLOCK)
    inside = jnp.where(diff >= 0, jnp.where(diff < ATTN_BLOCK, ki - first, -1), -1)
    mask = inside >= 0
    grp = lax.broadcasted_iota(jnp.int32, (1, GROUP * ATTN_BLOCK), 1) // ATTN_BLOCK
    ones = jnp.ones((2 * ATTN_BLOCK, HEAD_DIM), BF16)
    scores, sinks = [], []
    for h in range(N_KV_HEADS):
        kh = k[:, h * HEAD_DIM:(h + 1) * HEAD_DIM]
        qh = jnp.concatenate([q[:, (h * GROUP + g) * HEAD_DIM:(h * GROUP + g + 1) * HEAD_DIM]
                              for g in range(GROUP)], axis=0)
        sink = jnp.zeros((1, GROUP * ATTN_BLOCK), F32)
        for g in range(GROUP):
            sink = jnp.where(grp == g, sink_ref[h * GROUP + g], sink)
        sinks.append(sink)
        scores.append(lax.dot_general(kh, qh, (((1,), (1,)), ((), ())),
                                      preferred_element_type=F32))
    probs, sink_terms = [], []
    for h in range(N_KV_HEADS):
        s = jnp.where(mask, scores[h], NEG_INF)
        m = jnp.maximum(jnp.max(s, axis=0, keepdims=True), sinks[h])
        probs.append(jnp.exp(s - m).astype(BF16))
        sink_terms.append(jnp.exp(sinks[h] - m))
    outs = []
    for h in range(N_KV_HEADS):
        vh = jnp.concatenate([v[:, h * HEAD_DIM:(h + 1) * HEAD_DIM], ones], axis=-1)
        ov = lax.dot_general(vh, probs[h], (((0,), (0,)), ((), ())),
                             preferred_element_type=F32)
        denom = ov[HEAD_DIM:HEAD_DIM + 1, :] + sink_terms[h]
        o = (ov[:HEAD_DIM, :] / denom).T
        outs.extend(o[g * ATTN_BLOCK:(g + 1) * ATTN_BLOCK, :] for g in range(GROUP))
    o_ref[...] = jnp.concatenate(outs, axis=-1).astype(o_ref.dtype)


def attention(z, sinks, batch, seq):
    nb = seq // ATTN_BLOCK
    t = batch * seq
    cur = lambda b, n: b * nb + n
    prev = lambda b, n: b * nb + jnp.maximum(n - 1, 0)
    return pl.pallas_call(
        _attn_kernel,
        grid=(batch, nb),
        in_specs=[pl.BlockSpec(memory_space=pltpu.SMEM),
                  pl.BlockSpec((ATTN_BLOCK, D_MODEL), lambda b, n: (cur(b, n), COL_Q)),
                  pl.BlockSpec((ATTN_BLOCK, KV_W), lambda b, n: (cur(b, n), COL_K_KV)),
                  pl.BlockSpec((ATTN_BLOCK, KV_W), lambda b, n: (cur(b, n), COL_V_KV)),
                  pl.BlockSpec((ATTN_BLOCK, KV_W), lambda b, n: (prev(b, n), COL_K_KV)),
                  pl.BlockSpec((ATTN_BLOCK, KV_W), lambda b, n: (prev(b, n), COL_V_KV))],
        out_specs=pl.BlockSpec((ATTN_BLOCK, D_MODEL), lambda b, n: (cur(b, n), 0)),
        out_shape=jax.ShapeDtypeStruct((t, D_MODEL), BF16),
        compiler_params=_params(("arbitrary", "arbitrary")),
        name="swa_attention",
    )(sinks, z, z, z, z, z)


HALO = 8


def _convrec_kernel(cb_ref, cc_ref, cx_ref, rx_ref, ry_ref, cw_ref, rcw_ref, rcb_ref,
                    wr_ref, wi_ref, br_ref, bi_ref, lam_ref, conv_ref, rec_ref,
                    ubuf, xbuf, hcar):
    ts = cb_ref.shape[0]

    @pl.when(pl.program_id(1) == 0)
    def _():
        ubuf[0:HALO, :] = jnp.zeros((HALO, D_MODEL), F32)
        xbuf[0:HALO, :] = jnp.zeros((HALO, D_MODEL), F32)
        hcar[...] = jnp.zeros_like(hcar)

    u = cc_ref[...].astype(F32) * cx_ref[...].astype(F32)
    ubuf[HALO:HALO + ts, :] = u
    y = (cw_ref[2:3, :] * u + cw_ref[1:2, :] * ubuf[HALO - 1:HALO - 1 + ts, :]
         + cw_ref[0:1, :] * ubuf[HALO - 2:HALO - 2 + ts, :])
    conv_ref[...] = (cb_ref[...].astype(F32) * y).astype(conv_ref.dtype)
    ubuf[0:HALO, :] = ubuf[ts:ts + HALO, :]

    x = rx_ref[...].astype(F32)
    xbuf[HALO:HALO + ts, :] = x
    c = (rcw_ref[3:4, :] * x + rcw_ref[2:3, :] * xbuf[HALO - 1:HALO - 1 + ts, :]
         + rcw_ref[1:2, :] * xbuf[HALO - 2:HALO - 2 + ts, :]
         + rcw_ref[0:1, :] * xbuf[HALO - 3:HALO - 3 + ts, :] + rcb_ref[...])
    xbuf[0:HALO, :] = xbuf[ts:ts + HALO, :]
    cbf = c.astype(BF16)
    r_parts, i_parts = [], []
    for h in range(REC_HEADS):
        ch = cbf[:, h * REC_HEAD_DIM:(h + 1) * REC_HEAD_DIM]
        r_parts.append(jnp.dot(ch, wr_ref[h], preferred_element_type=F32))
        i_parts.append(jnp.dot(ch, wi_ref[h], preferred_element_type=F32))
    r = _sigmoid(jnp.concatenate(r_parts, axis=-1) + br_ref[...])
    ig = _sigmoid(jnp.concatenate(i_parts, axis=-1) + bi_ref[...])
    nl = -lam_ref[...]
    softplus = jnp.maximum(nl, 0.0) + jnp.log(1.0 + jnp.exp(-jnp.abs(nl)))
    log_a = -LRU_C * r * softplus
    a = jnp.exp(log_a)
    b = jnp.sqrt(1.0 - jnp.exp(2.0 * log_a)) * (ig * c)
    a3 = a.reshape(ts // 8, 8, D_MODEL)
    b3 = b.reshape(ts // 8, 8, D_MODEL)
    rid = lax.broadcasted_iota(jnp.int32, a3.shape, 1)
    for d in (1, 2, 4):
        inside = rid >= d
        a_sh = jnp.where(inside, pltpu.roll(a3, d, axis=1), 1.0)
        b_sh = jnp.where(inside, pltpu.roll(b3, d, axis=1), 0.0)
        b3 = a3 * b_sh + b3
        a3 = a3 * a_sh
    carry = hcar[0:1, :]
    groups = []
    for g in range(ts // 8):
        hg = b3[g] + a3[g] * carry
        groups.append(hg)
        carry = hg[7:8, :]
    hcar[0:1, :] = carry
    hs = jnp.concatenate(groups, axis=0)
    rec_ref[...] = (_gelu(ry_ref[...].astype(F32)) * hs).astype(rec_ref.dtype)


def conv_rec(z, cw, rcw, rcb, wr, wi, br, bi, lam, batch, seq, ts=256):
    ts = min(ts, seq)
    nt = seq // ts
    t = batch * seq
    row = lambda b, i: b * nt + i
    zspec = lambda col: pl.BlockSpec((ts, D_MODEL), lambda b, i: (row(b, i), col))
    full = lambda a: pl.BlockSpec(a.shape, lambda b, i: (0,) * a.ndim)
    return pl.pallas_call(
        _convrec_kernel,
        grid=(batch, nt),
        in_specs=[zspec(COL_CB), zspec(COL_CC), zspec(COL_CX), zspec(COL_RX), zspec(COL_RY),
                  full(cw), full(rcw), full(rcb), full(wr), full(wi), full(br), full(bi),
                  full(lam)],
        out_specs=[pl.BlockSpec((ts, D_MODEL), lambda b, i: (row(b, i), 0)),
                   pl.BlockSpec((ts, D_MODEL), lambda b, i: (row(b, i), 0))],
        out_shape=[jax.ShapeDtypeStruct((t, D_MODEL), BF16),
                   jax.ShapeDtypeStruct((t, D_MODEL), BF16)],
        scratch_shapes=[pltpu.VMEM((ts + HALO, D_MODEL), F32),
                        pltpu.VMEM((ts + HALO, D_MODEL), F32),
                        pltpu.VMEM((HALO, D_MODEL), F32)],
        compiler_params=_params(("arbitrary", "arbitrary"), VMEM_LIMIT),
        name="conv_rec",
    )(z, z, z, z, z, cw, rcw, rcb, wr, wi, br, bi, lam)


def _merge_kernel(a_ref, c_ref, r_ref, g0_ref, g1_ref, g2_ref, h_ref, wb_ref, wo_ref, o_ref):
    merged = None
    for n, (br, gt) in enumerate(((a_ref, g0_ref), (c_ref, g1_ref), (r_ref, g2_ref))):
        y = jnp.dot(br[...], wb_ref[n], preferred_element_type=F32)
        term = _sigmoid(gt[...].astype(F32)) * y
        merged = term if merged is None else merged + term
    o_ref[...] = h_ref[...] + jnp.dot(merged.astype(BF16), wo_ref[...],
                                      preferred_element_type=F32)


def merge(attn, conv, rec, z, h, wb, wo, tm=512):
    t = h.shape[0]
    tm = min(tm, t)
    blk = lambda col: pl.BlockSpec((tm, D_MODEL), lambda i: (i, col))
    return pl.pallas_call(
        _merge_kernel,
        grid=(t // tm,),
        in_specs=[blk(0), blk(0), blk(0), blk(COL_G0), blk(COL_G0 + 1), blk(COL_G0 + 2), blk(0),
                  pl.BlockSpec(wb.shape, lambda i: (0, 0, 0)),
                  pl.BlockSpec(wo.shape, lambda i: (0, 0))],
        out_specs=blk(0),
        out_shape=jax.ShapeDtypeStruct((t, D_MODEL), F32),
        compiler_params=_params(("arbitrary",), VMEM_LIMIT),
        name="merge",
    )(attn, conv, rec, z, z, z, h, wb, wo)


def _sorting_network(n):
    pairs = []
    p = 1
    while p < n:
        k = p
        while k >= 1:
            for j in range(k % p, n - k, 2 * k):
                for i in range(min(k, n - j - k)):
                    if (i + j) // (2 * p) == (i + j + k) // (2 * p):
                        pairs.append((i + j, i + j + k))
            k //= 2
        p *= 2
    return pairs


def _topk_columns(vals, tags, k):
    vals, tags = list(vals), list(tags)
    n = len(vals)
    for i, j in _sorting_network(n):
        va, ta, vb, tb = vals[i], tags[i], vals[j], tags[j]
        first = (va > vb) | ((va == vb) & (ta < tb))
        vals[i], vals[j] = jnp.where(first, va, vb), jnp.where(first, vb, va)
        tags[i], tags[j] = jnp.where(first, ta, tb), jnp.where(first, tb, ta)
    out_v, out_t = [], []
    for it in range(k):
        m = jnp.max(vals[0], axis=0, keepdims=True)
        best = jnp.min(jnp.where(vals[0] == m, tags[0], TAG_NONE), axis=0, keepdims=True)
        out_v.append(m)
        out_t.append(best)
        left = k - 1 - it
        pop = tags[0] == best
        for d in range(min(left, n - 1)):
            vals[d] = jnp.where(pop, vals[d + 1], vals[d])
            tags[d] = jnp.where(pop, tags[d + 1], tags[d])
        if left > n - 1:
            vals[n - 1] = jnp.where(pop, -jnp.inf, vals[n - 1])
            tags[n - 1] = jnp.where(pop, TAG_NONE, tags[n - 1])
    return out_v, out_t


def _stack_rows(rows, lo, hi):
    n = hi - lo
    rid = lax.broadcasted_iota(jnp.int32, (n, rows[0].shape[1]), 0)
    out = jnp.broadcast_to(rows[lo], (n, rows[0].shape[1]))
    for i in range(1, n):
        out = jnp.where(rid == i, rows[lo + i], out)
    return out


def _pair_candidates(s1, s2):
    k = PEER_TOPK
    tm = s1[0].shape[1]
    r = lax.broadcasted_iota(jnp.int32, (8, tm), 0)
    a_lo, a_hi = _stack_rows(s1, 0, 8), _stack_rows(s1, 8, 16)
    b_lo, b_hi = _stack_rows(s2, 0, 8), _stack_rows(s2, 8, 16)

    def slab(vals, tag, lo, hi):
        keep = jnp.where(r >= lo, r, hi + 1) <= hi
        return jnp.where(keep, vals, -jnp.inf), tag.astype(F32)

    slabs = [slab(a_lo + s2[0], r * k, 0, 7),
             slab(a_hi + s2[0], (r + 8) * k, 0, 7),
             slab(s1[0] + b_lo, r, 1, 7),
             slab(s1[0] + b_hi, r + 8, 0, 7),
             slab(a_lo + s2[1], r * k + 1, 1, 7),
             slab(s1[1] + b_lo, r + k, 2, 7),
             slab(a_lo + s2[2], r * k + 2, 2, 4),
             slab(a_lo + s2[3], r * k + 3, 2, 3),
             slab(a_lo + s2[4], r * k + 4, 2, 2)]
    return [v for v, _ in slabs], [t for _, t in slabs]


def _route_kernel(h_ref, g_ref, wq_ref, keys_ref, xn_ref, ids_ref, gates_ref,
                  q_ref, ids_t, gates_t):
    xn = _rms(h_ref[...], g_ref[...])
    xn_ref[...] = xn
    q_ref[...] = jnp.dot(xn.astype(BF16), wq_ref[...], preferred_element_type=F32)
    tm = h_ref.shape[0]
    row8 = lax.broadcasted_iota(jnp.int32, (8, tm), 0)
    key_tags = [(row8 + 8 * g).astype(F32) for g in range(N_KEYS // 8)]
    key_parts = [_split_bf16(keys_ref[p]) for p in range(2)]

    def one_head(hd):
        tops = []
        for p in range(2):
            c0 = pl.multiple_of((hd * 2 + p) * PEER_HALF, PEER_HALF)
            q_hi, q_lo = _split_bf16(q_ref[:, pl.ds(c0, PEER_HALF)])
            k_hi, k_lo = key_parts[p]
            nt = (((1,), (1,)), ((), ()))
            sc = (lax.dot_general(k_hi, q_hi, nt, preferred_element_type=F32)
                  + lax.dot_general(k_hi, q_lo, nt, preferred_element_type=F32)
                  + lax.dot_general(k_lo, q_hi, nt, preferred_element_type=F32))
            slabs = [sc[8 * g:8 * g + 8, :] for g in range(N_KEYS // 8)]
            tops.append(_topk_columns(slabs, key_tags, PEER_TOPK))
        (s1, i1), (s2, i2) = tops
        cand, flat = _pair_candidates(s1, s2)
        best_s, best_c = _topk_columns(cand, flat, PEER_TOPK)
        best_s = _stack_rows(best_s, 0, PEER_TOPK)
        flat_c = _stack_rows(best_c, 0, PEER_TOPK).astype(jnp.int32)
        a_idx = flat_c >> 4
        b_idx = flat_c & (PEER_TOPK - 1)
        e1 = jnp.zeros(flat_c.shape, F32)
        e2 = jnp.zeros(flat_c.shape, F32)
        for a in range(PEER_TOPK):
            e1 = jnp.where(a_idx == a, i1[a], e1)
            e2 = jnp.where(b_idx == a, i2[a], e2)
        ex = jnp.exp(best_s - best_s[0:1, :])
        r0 = pl.multiple_of(hd * PEER_TOPK, PEER_TOPK)
        gates_t[pl.ds(r0, PEER_TOPK), :] = ex / jnp.sum(ex, axis=0, keepdims=True)
        ids_t[pl.ds(r0, PEER_TOPK), :] = ((e1 * N_KEYS + e2) * SLAB).astype(jnp.int32)

    def heads(step, carry):
        for i in range(ROUTE_HEADS_PER_STEP):
            one_head(step * ROUTE_HEADS_PER_STEP + i)
        return carry

    lax.fori_loop(0, PEER_HEADS // ROUTE_HEADS_PER_STEP, heads, 0)
    ids_ref[...] = ids_t[...].T
    gates_ref[...] = gates_t[...].T


def route(h, g, wq, keys, tm=128):
    t = h.shape[0]
    return pl.pallas_call(
        _route_kernel,
        grid=(t // tm,),
        in_specs=[pl.BlockSpec((tm, D_MODEL), lambda i: (i, 0)),
                  pl.BlockSpec((1, D_MODEL), lambda i: (0, 0)),
                  pl.BlockSpec(wq.shape, lambda i: (0, 0)),
                  pl.BlockSpec(keys.shape, lambda i: (0, 0, 0))],
        out_specs=[pl.BlockSpec((tm, D_MODEL), lambda i: (i, 0)),
                   pl.BlockSpec((tm, N_PAIRS), lambda i: (i, 0)),
                   pl.BlockSpec((tm, N_PAIRS), lambda i: (i, 0))],
        out_shape=[jax.ShapeDtypeStruct((t, D_MODEL), F32),
                   jax.ShapeDtypeStruct((t, N_PAIRS), jnp.int32),
                   jax.ShapeDtypeStruct((t, N_PAIRS), F32)],
        scratch_shapes=[pltpu.VMEM((tm, 2 * PEER_HALF * PEER_HEADS), F32),
                        pltpu.VMEM((N_PAIRS, tm), jnp.int32),
                        pltpu.VMEM((N_PAIRS, tm), F32)],
        compiler_params=_params(("arbitrary",), VMEM_LIMIT),
        name="peer_route",
    )(h, g, wq, keys)


def _gather_tokens(ids_ref, tab_ref, tiles, t0):
    zero = lax.min(t0, 0)
    offs = [zero + u for u in range(INDEX_COLS)]
    for jj in range(N_PAIRS // INDEX_COLS):
        rows = [ids_ref.at[t0 + q, pl.ds(INDEX_COLS * jj, INDEX_COLS)] for q in range(len(tiles))]
        for u in range(INDEX_COLS):
            j = INDEX_COLS * jj + u
            for q, tile in enumerate(tiles):
                e = pl.multiple_of(rows[q][offs[u]], SLAB)
                tile[pl.ds(j, SLAB, stride=TILE_STRIDE), :] = tab_ref[pl.ds(e, SLAB), :]


def _pipelined_tokens(tb, tiles, consume, ids_ref, tab_ref):
    nfl = len(tiles)

    def body(k, carry):
        t0 = nfl * k
        for q in range(nfl):
            consume(jnp.maximum(t0 - nfl + q, 0), tiles[q])
        _gather_tokens(ids_ref, tab_ref, tiles, t0)
        return carry

    lax.fori_loop(0, tb // nfl, body, 0)
    for q in range(nfl):
        consume(tb - nfl + q, tiles[q])


def _tile_chunk(tile, s):
    return pltpu.bitcast(tile[pl.ds(s * TILE_STRIDE, N_PAIRS), :], BF16)


def _lane_parity(shape):
    lane = lax.broadcasted_iota(jnp.int32, shape, len(shape) - 1)
    return (lane & 1) == 1


def _peer_u_kernel(ids_ref, x_ref, gates_ref, tab_ref, w_ref, sc_ref, *tiles):
    tb = x_ref.shape[0]

    @pl.when(pl.program_id(0) == 0)
    def _():
        for tile in tiles:
            tile[...] = jnp.zeros_like(tile)

    def scores(t, tile):
        g = jnp.concatenate([_tile_chunk(tile, s) for s in range(SLAB)], axis=-1)
        xr = x_ref[pl.ds(t, 1), :]
        xa_hi, xa_lo = _split_bf16(xr[:, :ROW_WORDS])
        xb_hi, xb_lo = _split_bf16(xr[:, ROW_WORDS:])
        x8 = jnp.concatenate([xa_hi, xb_hi, xa_lo, xb_lo,
                              jnp.zeros((4, ROW_WORDS), BF16)], axis=0)
        res = lax.dot_general(x8, g, (((1,), (1,)), ((), ())),
                              preferred_element_type=F32)
        odd = _lane_parity((1, 2 * N_PAIRS))
        part = jnp.where(odd, res[1:2] + res[3:4], res[0:1] + res[2:3])
        sc_ref[pl.ds(t, 1), :] = part + pltpu.roll(part, 1, axis=1)

    _pipelined_tokens(tb, tiles, scores, ids_ref, tab_ref)

    r = lax.broadcasted_iota(jnp.int32, (N_PAIRS, 2 * N_PAIRS), 0)
    c = lax.broadcasted_iota(jnp.int32, (N_PAIRS, 2 * N_PAIRS), 1)
    spread = jnp.where(c == 2 * r + 1, 1.0, 0.0).astype(BF16)
    g_hi, g_lo = _split_bf16(gates_ref[...])
    g2 = (jnp.dot(g_hi, spread, preferred_element_type=F32)
          + jnp.dot(g_lo, spread, preferred_element_type=F32))
    w_ref[...] = g2 * _gelu(sc_ref[...])


def _table_spec():
    return pl.BlockSpec((N_EXPERTS * SLAB, LANES), lambda i: (0, 0),
                        pipeline_mode=pl.Buffered(1))


def _tile_scratch(n):
    return [pltpu.VMEM((SLAB * TILE_STRIDE, LANES), jnp.int32) for _ in range(n)]


def peer_scores(ids, xn, gates, tab, tb=512):
    t = xn.shape[0]
    return pl.pallas_call(
        _peer_u_kernel,
        grid=(t // tb,),
        in_specs=[pl.BlockSpec((tb, N_PAIRS), lambda i: (i, 0), memory_space=pltpu.SMEM),
                  pl.BlockSpec((tb, D_MODEL), lambda i: (i, 0)),
                  pl.BlockSpec((tb, N_PAIRS), lambda i: (i, 0)),
                  _table_spec()],
        out_specs=pl.BlockSpec((tb, 2 * N_PAIRS), lambda i: (i, 0)),
        out_shape=jax.ShapeDtypeStruct((t, 2 * N_PAIRS), F32),
        scratch_shapes=([pltpu.VMEM((tb, 2 * N_PAIRS), F32)]
                        + _tile_scratch(SCORES_TOKENS_IN_FLIGHT)),
        compiler_params=_params(("arbitrary",), VMEM_LIMIT),
        name="peer_u",
    )(ids, xn, gates, tab)


def _peer_v_kernel(ids_ref, w_ref, h_ref, tab_ref, o_ref, acc_ref, lhs_refs, *tiles):
    tb = h_ref.shape[0]

    @pl.when(pl.program_id(0) == 0)
    def _():
        for tile in tiles:
            tile[...] = jnp.zeros_like(tile)

    w_odd = w_ref[...]
    w_even = pltpu.roll(w_odd, 2 * N_PAIRS - 1, axis=1)
    for i, w in enumerate((w_even, w_odd)):
        hi = w.astype(BF16).astype(F32)
        lhs_refs[i, :, :] = hi
        lhs_refs[2 + i, :, :] = w - hi

    def combine(t, tile):
        rid = lax.broadcasted_iota(jnp.int32, (8, 2 * N_PAIRS), 0)
        lhs = jnp.zeros((8, 2 * N_PAIRS), F32)
        for i in range(4):
            row = lhs_refs[i, pl.ds(t, 1), :]
            lhs = jnp.where(rid == i, row, lhs)
        lhs = lhs.astype(BF16)
        lo, hi = [], []
        for s in range(SLAB):
            res = jnp.dot(lhs, _tile_chunk(tile, s), preferred_element_type=F32)
            lo.append(res[0:1] + res[2:3])
            hi.append(res[1:2] + res[3:4])
        acc_ref[pl.ds(t, 1), :] = jnp.concatenate(lo + hi, axis=-1)

    _pipelined_tokens(tb, tiles, combine, ids_ref, tab_ref)
    o_ref[...] = h_ref[...] + acc_ref[...]


def peer_combine(ids, w, h, tab, tb=512):
    t = h.shape[0]
    return pl.pallas_call(
        _peer_v_kernel,
        grid=(t // tb,),
        in_specs=[pl.BlockSpec((tb, N_PAIRS), lambda i: (i, 0), memory_space=pltpu.SMEM),
                  pl.BlockSpec((tb, 2 * N_PAIRS), lambda i: (i, 0)),
                  pl.BlockSpec((tb, D_MODEL), lambda i: (i, 0)),
                  _table_spec()],
        out_specs=pl.BlockSpec((tb, D_MODEL), lambda i: (i, 0)),
        out_shape=jax.ShapeDtypeStruct((t, D_MODEL), F32),
        scratch_shapes=[pltpu.VMEM((tb, D_MODEL), F32),
                        pltpu.VMEM((4, tb, 2 * N_PAIRS), F32)]
                       + _tile_scratch(COMBINE_TOKENS_IN_FLIGHT),
        compiler_params=_params(("arbitrary",), VMEM_LIMIT),
        name="peer_v",
    )(ids, w, h, tab)


def _ple_kernel(h_ref, p_ref, g_ref, wg_ref, wp_ref, gf_ref, o_ref, *, final):
    h = h_ref[...]
    gate = _sigmoid(jnp.dot(_rms(h, g_ref[...]).astype(BF16), wg_ref[...],
                            preferred_element_type=F32))
    out = h + gate * jnp.dot(p_ref[...].astype(BF16), wp_ref[...],
                             preferred_element_type=F32)
    if final:
        out = _rms(out, gf_ref[...])
    o_ref[...] = out


def ple(h, p, layer, g, wg, wp, gf, final, tm=512):
    t = h.shape[0]
    tm = min(tm, t)
    return pl.pallas_call(
        functools.partial(_ple_kernel, final=final),
        grid=(t // tm,),
        in_specs=[pl.BlockSpec((tm, D_MODEL), lambda i: (i, 0)),
                  pl.BlockSpec((None, tm, PLE_DIM), lambda i: (layer, i, 0)),
                  pl.BlockSpec((1, D_MODEL), lambda i: (0, 0)),
                  pl.BlockSpec(wg.shape, lambda i: (0, 0)),
                  pl.BlockSpec(wp.shape, lambda i: (0, 0)),
                  pl.BlockSpec((1, D_MODEL), lambda i: (0, 0))],
        out_specs=pl.BlockSpec((tm, D_MODEL), lambda i: (i, 0)),
        out_shape=jax.ShapeDtypeStruct((t, D_MODEL), F32),
        compiler_params=_params(("arbitrary",), VMEM_LIMIT),
        name="ple",
    )(h, p, g, wg, wp, gf)


def _cast_kernel(w_ref, o_ref):
    o_ref[...] = w_ref[...].astype(o_ref.dtype)


def _permute_w_in(w_in, layer, tn=2 * KV_W):
    nb = IN_COLS // tn
    q_blocks, kv_blocks = D_MODEL // tn, 2 * KV_W // tn

    def src(j):
        return jnp.where(j < q_blocks, j,
                         jnp.where(j < nb - kv_blocks, j + kv_blocks, j - (nb - kv_blocks) + q_blocks))

    return pl.pallas_call(
        _cast_kernel,
        grid=(nb,),
        in_specs=[pl.BlockSpec((None, D_MODEL, tn), lambda j: (layer, 0, src(j)))],
        out_specs=pl.BlockSpec((D_MODEL, tn), lambda j: (0, j)),
        out_shape=jax.ShapeDtypeStruct((D_MODEL, IN_COLS), BF16),
        compiler_params=_params(("arbitrary",)),
        name="cast_w_in",
    )(w_in)


def _bf16_high_bits(x):
    bits = lax.bitcast_convert_type(x, jnp.uint32)
    return bits + (jnp.uint32(0x7FFF) + ((bits >> 16) & jnp.uint32(1)))


def _pack_kernel(t_ref, o_ref):
    x = t_ref[...]
    rows = x.shape[0]
    for s in range(SLAB):
        lo = _bf16_high_bits(x[:, s * LANES:(s + 1) * LANES])
        hi = _bf16_high_bits(x[:, ROW_WORDS + s * LANES:ROW_WORDS + (s + 1) * LANES])
        words = (lo >> 16) | (hi & jnp.uint32(0xFFFF0000))
        o_ref[pl.ds(s, rows, stride=SLAB), :] = lax.bitcast_convert_type(words, jnp.int32)


def _pack_table(tabs, layer, te=512):
    return pl.pallas_call(
        _pack_kernel,
        grid=(N_EXPERTS // te,),
        in_specs=[pl.BlockSpec((None, te, D_MODEL), lambda i: (layer, i, 0))],
        out_specs=pl.BlockSpec((te * SLAB, LANES), lambda i: (i, 0)),
        out_shape=jax.ShapeDtypeStruct((N_EXPERTS * SLAB, LANES), jnp.int32),
        compiler_params=_params(("arbitrary",)),
        name="pack_table",
    )(tabs)


def kernel(x, p, norm_mix, w_in, attn_sinks, conv_w, rec_conv_w, rec_conv_b, w_rgate, b_rgate, w_igate, b_igate, lru_lambda, w_branch, w_out, norm_ffn, w_peer_q, peer_sub_keys, peer_u, peer_v, norm_ple, w_ple_gate, w_ple_proj, norm_final):
    batch, seq, d = x.shape
    depth = w_in.shape[0]
    t = batch * seq
    row = lambda a: a.reshape(1, -1)
    h = x.reshape(t, d)
    p3 = p.reshape(depth, t, -1)
    for l in range(depth):
        z = norm_matmul(h, row(norm_mix[l]), _permute_w_in(w_in, l))
        attn = attention(z, attn_sinks[l], batch, seq)
        conv, rec = conv_rec(z, conv_w[l], rec_conv_w[l], row(rec_conv_b[l]),
                             w_rgate[l].astype(BF16), w_igate[l].astype(BF16),
                             row(b_rgate[l]), row(b_igate[l]), row(lru_lambda[l]), batch, seq)
        h = merge(attn, conv, rec, z, h, w_branch[l].astype(BF16), w_out[l].astype(BF16))
        xn, ids, gates = route(h, row(norm_ffn[l]), w_peer_q[l].astype(BF16), peer_sub_keys[l])
        w = peer_scores(ids, xn, gates, _pack_table(peer_u, l))
        h = peer_combine(ids, w, h, _pack_table(peer_v, l))
        h = ple(h, p3, l, row(norm_ple[l]), w_ple_gate[l].astype(BF16),
                w_ple_proj[l].astype(BF16), row(norm_final), final=(l == depth - 1))
    return h.reshape(batch, seq, d)
```

```python
import functools

import jax
import jax.numpy as jnp
from jax import lax
from jax.experimental import pallas as pl
from jax.experimental.pallas import tpu as pltpu

F32 = jnp.float32
BF16 = jnp.bfloat16

D_MODEL = 1024
N_Q_HEADS = 16
N_KV_HEADS = 4
HEAD_DIM = 64
GROUP = N_Q_HEADS // N_KV_HEADS
ATTN_BLOCK = 128
KV_W = N_KV_HEADS * HEAD_DIM
REC_HEADS = 4
REC_HEAD_DIM = D_MODEL // REC_HEADS
LRU_C = 8.0
N_KEYS = 128
N_EXPERTS = N_KEYS * N_KEYS
PEER_HEADS = 8
PEER_TOPK = 16
PEER_HALF = 128
N_PAIRS = PEER_HEADS * PEER_TOPK
PLE_DIM = 256
EPS = 1e-6
NEG_INF = -1e30
TAG_NONE = float(1 << 20)
ROUTE_LANES = 128

COL_Q, COL_CB, COL_CC, COL_CX, COL_RX, COL_RY, COL_G0 = 0, 1, 2, 3, 4, 5, 6
IN_COLS = 9 * D_MODEL + 2 * KV_W
COL_K_KV = 9 * D_MODEL // KV_W
COL_V_KV = COL_K_KV + 1

LANES = 128
SLAB = D_MODEL // 2 // LANES
ROW_WORDS = SLAB * LANES
TILE_STRIDE = N_PAIRS + 8
INDEX_COLS = 8
SCORES_TOKENS_IN_FLIGHT = 16
COMBINE_TOKENS_IN_FLIGHT = 16
VMEM_LIMIT = 56 * 1024 * 1024


def _params(sem, vmem=None):
    return pltpu.CompilerParams(dimension_semantics=sem, vmem_limit_bytes=vmem)


def _rms(x32, g):
    ms = jnp.mean(x32 * x32, axis=-1, keepdims=True)
    return x32 * lax.rsqrt(ms + EPS) * g


def _sigmoid(x):
    return 0.5 * jnp.tanh(0.5 * x) + 0.5


def _gelu(x):
    return 0.5 * x * (1.0 + jnp.tanh(0.7978845608028654 * (x + 0.044715 * (x * x * x))))


def _split_bf16(x32):
    hi = x32.astype(BF16)
    lo = (x32 - hi.astype(F32)).astype(BF16)
    return hi, lo


def _norm_matmul_kernel(x_ref, g_ref, w_ref, o_ref, xn_ref):
    @pl.when(pl.program_id(1) == 0)
    def _():
        xn_ref[...] = _rms(x_ref[...], g_ref[...]).astype(BF16)

    o_ref[...] = jnp.dot(xn_ref[...], w_ref[...],
                         preferred_element_type=F32).astype(o_ref.dtype)


def norm_matmul(x, g, w, tm=2048, tn=1024):
    m, d = x.shape
    n = w.shape[1]
    tm = min(tm, m)
    return pl.pallas_call(
        _norm_matmul_kernel,
        grid=(m // tm, pl.cdiv(n, tn)),
        in_specs=[pl.BlockSpec((tm, d), lambda i, j: (i, 0)),
                  pl.BlockSpec((1, d), lambda i, j: (0, 0)),
                  pl.BlockSpec((d, tn), lambda i, j: (0, j))],
        out_specs=pl.BlockSpec((tm, tn), lambda i, j: (i, j)),
        out_shape=jax.ShapeDtypeStruct((m, n), BF16),
        scratch_shapes=[pltpu.VMEM((tm, d), BF16)],
        compiler_params=_params(("arbitrary", "arbitrary"), VMEM_LIMIT),
        name="norm_matmul",
    )(x, g, w)


def _attn_kernel(sink_ref, q_ref, kc_ref, vc_ref, kp_ref, vp_ref, o_ref):
    n = pl.program_id(1)
    q = q_ref[...] * (HEAD_DIM ** -0.5)
    k = jnp.concatenate([kp_ref[...], kc_ref[...]], axis=0)
    v = jnp.concatenate([vp_ref[...], vc_ref[...]], axis=0)
    shape = (2 * ATTN_BLOCK, GROUP * ATTN_BLOCK)
    ki = lax.broadcasted_iota(jnp.int32, shape, 0)
    col = lax.broadcasted_iota(jnp.int32, shape, 1)
    diff = (col & (ATTN_BLOCK - 1)) + ATTN_BLOCK - ki
    first = jnp.where(n > 0, 0, ATTN_BLOCK)
    inside = jnp.where(diff >= 0, jnp.where(diff < ATTN_BLOCK, ki - first, -1), -1)
    mask = inside >= 0
    grp = lax.broadcasted_iota(jnp.int32, (1, GROUP * ATTN_BLOCK), 1) // ATTN_BLOCK
    ones = jnp.ones((2 * ATTN_BLOCK, HEAD_DIM), BF16)
    scores, sinks = [], []
    for h in range(N_KV_HEADS):
        kh = k[:, h * HEAD_DIM:(h + 1) * HEAD_DIM]
        qh = jnp.concatenate([q[:, (h * GROUP + g) * HEAD_DIM:(h * GROUP + g + 1) * HEAD_DIM]
                              for g in range(GROUP)], axis=0)
        sink = jnp.zeros((1, GROUP * ATTN_BLOCK), F32)
        for g in range(GROUP):
            sink = jnp.where(grp == g, sink_ref[h * GROUP + g], sink)
        sinks.append(sink)
        scores.append(lax.dot_general(kh, qh, (((1,), (1,)), ((), ())),
                                      preferred_element_type=F32))
    probs, sink_terms = [], []
    for h in range(N_KV_HEADS):
        s = jnp.where(mask, scores[h], NEG_INF)
        m = jnp.maximum(jnp.max(s, axis=0, keepdims=True), sinks[h])
        probs.append(jnp.exp(s - m).astype(BF16))
        sink_terms.append(jnp.exp(sinks[h] - m))
    outs = []
    for h in range(N_KV_HEADS):
        vh = jnp.concatenate([v[:, h * HEAD_DIM:(h + 1) * HEAD_DIM], ones], axis=-1)
        ov = lax.dot_general(vh, probs[h], (((0,), (0,)), ((), ())),
                             preferred_element_type=F32)
        denom = ov[HEAD_DIM:HEAD_DIM + 1, :] + sink_terms[h]
        o = (ov[:HEAD_DIM, :] / denom).T
        outs.extend(o[g * ATTN_BLOCK:(g + 1) * ATTN_BLOCK, :] for g in range(GROUP))
    o_ref[...] = jnp.concatenate(outs, axis=-1).astype(o_ref.dtype)


def attention(z, sinks, batch, seq):
    nb = seq // ATTN_BLOCK
    t = batch * seq
    cur = lambda b, n: b * nb + n
    prev = lambda b, n: b * nb + jnp.maximum(n - 1, 0)
    return pl.pallas_call(
        _attn_kernel,
        grid=(batch, nb),
        in_specs=[pl.BlockSpec(memory_space=pltpu.SMEM),
                  pl.BlockSpec((ATTN_BLOCK, D_MODEL), lambda b, n: (cur(b, n), COL_Q)),
                  pl.BlockSpec((ATTN_BLOCK, KV_W), lambda b, n: (cur(b, n), COL_K_KV)),
                  pl.BlockSpec((ATTN_BLOCK, KV_W), lambda b, n: (cur(b, n), COL_V_KV)),
                  pl.BlockSpec((ATTN_BLOCK, KV_W), lambda b, n: (prev(b, n), COL_K_KV)),
                  pl.BlockSpec((ATTN_BLOCK, KV_W), lambda b, n: (prev(b, n), COL_V_KV))],
        out_specs=pl.BlockSpec((ATTN_BLOCK, D_MODEL), lambda b, n: (cur(b, n), 0)),
        out_shape=jax.ShapeDtypeStruct((t, D_MODEL), BF16),
        compiler_params=_params(("arbitrary", "arbitrary")),
        name="swa_attention",
    )(sinks, z, z, z, z, z)


HALO = 8


def _convrec_kernel(cb_ref, cc_ref, cx_ref, rx_ref, ry_ref, cw_ref, rcw_ref, rcb_ref,
                    wr_ref, wi_ref, br_ref, bi_ref, lam_ref, conv_ref, rec_ref,
                    ubuf, xbuf, hcar):
    ts = cb_ref.shape[0]

    @pl.when(pl.program_id(1) == 0)
    def _():
        ubuf[0:HALO, :] = jnp.zeros((HALO, D_MODEL), F32)
        xbuf[0:HALO, :] = jnp.zeros((HALO, D_MODEL), F32)
        hcar[...] = jnp.zeros_like(hcar)

    u = cc_ref[...].astype(F32) * cx_ref[...].astype(F32)
    ubuf[HALO:HALO + ts, :] = u
    y = (cw_ref[2:3, :] * u + cw_ref[1:2, :] * ubuf[HALO - 1:HALO - 1 + ts, :]
         + cw_ref[0:1, :] * ubuf[HALO - 2:HALO - 2 + ts, :])
    conv_ref[...] = (cb_ref[...].astype(F32) * y).astype(conv_ref.dtype)
    ubuf[0:HALO, :] = ubuf[ts:ts + HALO, :]

    x = rx_ref[...].astype(F32)
    xbuf[HALO:HALO + ts, :] = x
    c = (rcw_ref[3:4, :] * x + rcw_ref[2:3, :] * xbuf[HALO - 1:HALO - 1 + ts, :]
         + rcw_ref[1:2, :] * xbuf[HALO - 2:HALO - 2 + ts, :]
         + rcw_ref[0:1, :] * xbuf[HALO - 3:HALO - 3 + ts, :] + rcb_ref[...])
    xbuf[0:HALO, :] = xbuf[ts:ts + HALO, :]
    cbf = c.astype(BF16)
    r_parts, i_parts = [], []
    for h in range(REC_HEADS):
        ch = cbf[:, h * REC_HEAD_DIM:(h + 1) * REC_HEAD_DIM]
        r_parts.append(jnp.dot(ch, wr_ref[h], preferred_element_type=F32))
        i_parts.append(jnp.dot(ch, wi_ref[h], preferred_element_type=F32))
    r = _sigmoid(jnp.concatenate(r_parts, axis=-1) + br_ref[...])
    ig = _sigmoid(jnp.concatenate(i_parts, axis=-1) + bi_ref[...])
    nl = -lam_ref[...]
    softplus = jnp.maximum(nl, 0.0) + jnp.log(1.0 + jnp.exp(-jnp.abs(nl)))
    log_a = -LRU_C * r * softplus
    a = jnp.exp(log_a)
    b = jnp.sqrt(1.0 - jnp.exp(2.0 * log_a)) * (ig * c)
    a3 = a.reshape(ts // 8, 8, D_MODEL)
    b3 = b.reshape(ts // 8, 8, D_MODEL)
    rid = lax.broadcasted_iota(jnp.int32, a3.shape, 1)
    for d in (1, 2, 4):
        inside = rid >= d
        a_sh = jnp.where(inside, pltpu.roll(a3, d, axis=1), 1.0)
        b_sh = jnp.where(inside, pltpu.roll(b3, d, axis=1), 0.0)
        b3 = a3 * b_sh + b3
        a3 = a3 * a_sh
    carry = hcar[0:1, :]
    groups = []
    for g in range(ts // 8):
        hg = b3[g] + a3[g] * carry
        groups.append(hg)
        carry = hg[7:8, :]
    hcar[0:1, :] = carry
    hs = jnp.concatenate(groups, axis=0)
    rec_ref[...] = (_gelu(ry_ref[...].astype(F32)) * hs).astype(rec_ref.dtype)


def conv_rec(z, cw, rcw, rcb, wr, wi, br, bi, lam, batch, seq, ts=512):
    ts = min(ts, seq)
    nt = seq // ts
    t = batch * seq
    row = lambda b, i: b * nt + i
    zspec = lambda col: pl.BlockSpec((ts, D_MODEL), lambda b, i: (row(b, i), col))
    full = lambda a: pl.BlockSpec(a.shape, lambda b, i: (0,) * a.ndim)
    return pl.pallas_call(
        _convrec_kernel,
        grid=(batch, nt),
        in_specs=[zspec(COL_CB), zspec(COL_CC), zspec(COL_CX), zspec(COL_RX), zspec(COL_RY),
                  full(cw), full(rcw), full(rcb), full(wr), full(wi), full(br), full(bi),
                  full(lam)],
        out_specs=[pl.BlockSpec((ts, D_MODEL), lambda b, i: (row(b, i), 0)),
                   pl.BlockSpec((ts, D_MODEL), lambda b, i: (row(b, i), 0))],
        out_shape=[jax.ShapeDtypeStruct((t, D_MODEL), BF16),
                   jax.ShapeDtypeStruct((t, D_MODEL), BF16)],
        scratch_shapes=[pltpu.VMEM((ts + HALO, D_MODEL), F32),
                        pltpu.VMEM((ts + HALO, D_MODEL), F32),
                        pltpu.VMEM((HALO, D_MODEL), F32)],
        compiler_params=_params(("arbitrary", "arbitrary"), VMEM_LIMIT),
        name="conv_rec",
    )(z, z, z, z, z, cw, rcw, rcb, wr, wi, br, bi, lam)


def _merge_kernel(a_ref, c_ref, r_ref, g0_ref, g1_ref, g2_ref, h_ref, wb_ref, wo_ref, o_ref):
    merged = None
    for n, (br, gt) in enumerate(((a_ref, g0_ref), (c_ref, g1_ref), (r_ref, g2_ref))):
        y = jnp.dot(br[...], wb_ref[n], preferred_element_type=F32)
        term = _sigmoid(gt[...].astype(F32)) * y
        merged = term if merged is None else merged + term
    o_ref[...] = h_ref[...] + jnp.dot(merged.astype(BF16), wo_ref[...],
                                      preferred_element_type=F32)


def merge(attn, conv, rec, z, h, wb, wo, tm=512):
    t = h.shape[0]
    tm = min(tm, t)
    blk = lambda col: pl.BlockSpec((tm, D_MODEL), lambda i: (i, col))
    return pl.pallas_call(
        _merge_kernel,
        grid=(t // tm,),
        in_specs=[blk(0), blk(0), blk(0), blk(COL_G0), blk(COL_G0 + 1), blk(COL_G0 + 2), blk(0),
                  pl.BlockSpec(wb.shape, lambda i: (0, 0, 0)),
                  pl.BlockSpec(wo.shape, lambda i: (0, 0))],
        out_specs=blk(0),
        out_shape=jax.ShapeDtypeStruct((t, D_MODEL), F32),
        compiler_params=_params(("arbitrary",), VMEM_LIMIT),
        name="merge",
    )(attn, conv, rec, z, z, z, h, wb, wo)


def _sorting_network(n):
    pairs = []
    p = 1
    while p < n:
        k = p
        while k >= 1:
            for j in range(k % p, n - k, 2 * k):
                for i in range(min(k, n - j - k)):
                    if (i + j) // (2 * p) == (i + j + k) // (2 * p):
                        pairs.append((i + j, i + j + k))
            k //= 2
        p *= 2
    return pairs


def _topk_columns(vals, tags, k):
    vals, tags = list(vals), list(tags)
    n = len(vals)
    for i, j in _sorting_network(n):
        va, ta, vb, tb = vals[i], tags[i], vals[j], tags[j]
        first = (va > vb) | ((va == vb) & (ta < tb))
        vals[i], vals[j] = jnp.where(first, va, vb), jnp.where(first, vb, va)
        tags[i], tags[j] = jnp.where(first, ta, tb), jnp.where(first, tb, ta)
    out_v, out_t = [], []
    for it in range(k):
        m = jnp.max(vals[0], axis=0, keepdims=True)
        best = jnp.min(jnp.where(vals[0] == m, tags[0], TAG_NONE), axis=0, keepdims=True)
        out_v.append(m)
        out_t.append(best)
        left = k - 1 - it
        pop = tags[0] == best
        for d in range(min(left, n - 1)):
            vals[d] = jnp.where(pop, vals[d + 1], vals[d])
            tags[d] = jnp.where(pop, tags[d + 1], tags[d])
        if left > n - 1:
            vals[n - 1] = jnp.where(pop, -jnp.inf, vals[n - 1])
            tags[n - 1] = jnp.where(pop, TAG_NONE, tags[n - 1])
    return out_v, out_t


def _stack_rows(rows, lo, hi):
    n = hi - lo
    rid = lax.broadcasted_iota(jnp.int32, (n, rows[0].shape[1]), 0)
    out = jnp.broadcast_to(rows[lo], (n, rows[0].shape[1]))
    for i in range(1, n):
        out = jnp.where(rid == i, rows[lo + i], out)
    return out


def _pair_candidates(s1, s2):
    k = PEER_TOPK
    tm = s1[0].shape[1]
    r = lax.broadcasted_iota(jnp.int32, (8, tm), 0)
    a_lo, a_hi = _stack_rows(s1, 0, 8), _stack_rows(s1, 8, 16)
    b_lo, b_hi = _stack_rows(s2, 0, 8), _stack_rows(s2, 8, 16)

    def slab(vals, tag, lo, hi):
        keep = jnp.where(r >= lo, r, hi + 1) <= hi
        return jnp.where(keep, vals, -jnp.inf), tag.astype(F32)

    slabs = [slab(a_lo + s2[0], r * k, 0, 7),
             slab(a_hi + s2[0], (r + 8) * k, 0, 7),
             slab(s1[0] + b_lo, r, 1, 7),
             slab(s1[0] + b_hi, r + 8, 0, 7),
             slab(a_lo + s2[1], r * k + 1, 1, 7),
             slab(s1[1] + b_lo, r + k, 2, 7),
             slab(a_lo + s2[2], r * k + 2, 2, 4),
             slab(a_lo + s2[3], r * k + 3, 2, 3),
             slab(a_lo + s2[4], r * k + 4, 2, 2)]
    return [v for v, _ in slabs], [t for _, t in slabs]


def _route_kernel(h_ref, g_ref, wq_ref, keys_ref, xn_ref, ids_ref, gates_ref,
                  q_ref, ids_t, gates_t):
    xn = _rms(h_ref[...], g_ref[...])
    xn_ref[...] = xn
    q_ref[...] = jnp.dot(xn.astype(BF16), wq_ref[...], preferred_element_type=F32)
    row8 = lax.broadcasted_iota(jnp.int32, (8, ROUTE_LANES), 0)
    key_tags = [(row8 + 8 * g).astype(F32) for g in range(N_KEYS // 8)]
    key_parts = [_split_bf16(keys_ref[p]) for p in range(2)]

    def one_head(hd, t0):
        tops = []
        for p in range(2):
            c0 = (hd * 2 + p) * PEER_HALF
            q_hi, q_lo = _split_bf16(q_ref[pl.ds(t0, ROUTE_LANES), c0:c0 + PEER_HALF])
            k_hi, k_lo = key_parts[p]
            nt = (((1,), (1,)), ((), ()))
            sc = (lax.dot_general(k_hi, q_hi, nt, preferred_element_type=F32)
                  + lax.dot_general(k_hi, q_lo, nt, preferred_element_type=F32)
                  + lax.dot_general(k_lo, q_hi, nt, preferred_element_type=F32))
            slabs = [sc[8 * g:8 * g + 8, :] for g in range(N_KEYS // 8)]
            tops.append(_topk_columns(slabs, key_tags, PEER_TOPK))
        (s1, i1), (s2, i2) = tops
        cand, flat = _pair_candidates(s1, s2)
        best_s, best_c = _topk_columns(cand, flat, PEER_TOPK)
        best_s = _stack_rows(best_s, 0, PEER_TOPK)
        flat_c = _stack_rows(best_c, 0, PEER_TOPK).astype(jnp.int32)
        a_idx = flat_c >> 4
        b_idx = flat_c & (PEER_TOPK - 1)
        e1 = jnp.zeros(flat_c.shape, F32)
        e2 = jnp.zeros(flat_c.shape, F32)
        for a in range(PEER_TOPK):
            e1 = jnp.where(a_idx == a, i1[a], e1)
            e2 = jnp.where(b_idx == a, i2[a], e2)
        ex = jnp.exp(best_s - best_s[0:1, :])
        r0 = hd * PEER_TOPK
        gates_t[r0:r0 + PEER_TOPK, :] = ex / jnp.sum(ex, axis=0, keepdims=True)
        ids_t[r0:r0 + PEER_TOPK, :] = ((e1 * N_KEYS + e2) * SLAB).astype(jnp.int32)

    def token_group(i, carry):
        t0 = pl.multiple_of(i * ROUTE_LANES, ROUTE_LANES)
        for hd in range(PEER_HEADS):
            one_head(hd, t0)
        ids_ref[pl.ds(t0, ROUTE_LANES), :] = ids_t[...].T
        gates_ref[pl.ds(t0, ROUTE_LANES), :] = gates_t[...].T
        return carry

    lax.fori_loop(0, h_ref.shape[0] // ROUTE_LANES, token_group, 0)


def route(h, g, wq, keys, tm=512):
    t = h.shape[0]
    tm = min(tm, t)
    return pl.pallas_call(
        _route_kernel,
        grid=(t // tm,),
        in_specs=[pl.BlockSpec((tm, D_MODEL), lambda i: (i, 0)),
                  pl.BlockSpec((1, D_MODEL), lambda i: (0, 0)),
                  pl.BlockSpec(wq.shape, lambda i: (0, 0)),
                  pl.BlockSpec(keys.shape, lambda i: (0, 0, 0))],
        out_specs=[pl.BlockSpec((tm, D_MODEL), lambda i: (i, 0)),
                   pl.BlockSpec((tm, N_PAIRS), lambda i: (i, 0)),
                   pl.BlockSpec((tm, N_PAIRS), lambda i: (i, 0))],
        out_shape=[jax.ShapeDtypeStruct((t, D_MODEL), F32),
                   jax.ShapeDtypeStruct((t, N_PAIRS), jnp.int32),
                   jax.ShapeDtypeStruct((t, N_PAIRS), F32)],
        scratch_shapes=[pltpu.VMEM((tm, 2 * PEER_HALF * PEER_HEADS), F32),
                        pltpu.VMEM((N_PAIRS, ROUTE_LANES), jnp.int32),
                        pltpu.VMEM((N_PAIRS, ROUTE_LANES), F32)],
        compiler_params=_params(("arbitrary",), VMEM_LIMIT),
        name="peer_route",
    )(h, g, wq, keys)


def _gather_tokens(ids_ref, tab_ref, tiles, t0):
    zero = lax.min(t0, 0)
    offs = [zero + u for u in range(INDEX_COLS)]
    for jj in range(N_PAIRS // INDEX_COLS):
        rows = [ids_ref.at[t0 + q, pl.ds(INDEX_COLS * jj, INDEX_COLS)] for q in range(len(tiles))]
        for u in range(INDEX_COLS):
            j = INDEX_COLS * jj + u
            for q, tile in enumerate(tiles):
                e = pl.multiple_of(rows[q][offs[u]], SLAB)
                tile[pl.ds(j, SLAB, stride=TILE_STRIDE), :] = tab_ref[pl.ds(e, SLAB), :]


def _pipelined_tokens(tb, tiles, consume, ids_ref, tab_ref):
    nfl = len(tiles)

    def body(k, carry):
        t0 = nfl * k
        for q in range(nfl):
            consume(jnp.maximum(t0 - nfl + q, 0), tiles[q])
        _gather_tokens(ids_ref, tab_ref, tiles, t0)
        return carry

    lax.fori_loop(0, tb // nfl, body, 0)
    for q in range(nfl):
        consume(tb - nfl + q, tiles[q])


def _tile_chunk(tile, s):
    return pltpu.bitcast(tile[pl.ds(s * TILE_STRIDE, N_PAIRS), :], BF16)


def _lane_parity(shape):
    lane = lax.broadcasted_iota(jnp.int32, shape, len(shape) - 1)
    return (lane & 1) == 1


def _peer_u_kernel(ids_ref, x_ref, gates_ref, tab_ref, w_ref, sc_ref, *tiles):
    tb = x_ref.shape[0]

    @pl.when(pl.program_id(0) == 0)
    def _():
        for tile in tiles:
            tile[...] = jnp.zeros_like(tile)

    def scores(t, tile):
        g = jnp.concatenate([_tile_chunk(tile, s) for s in range(SLAB)], axis=-1)
        xr = x_ref[pl.ds(t, 1), :]
        xa_hi, xa_lo = _split_bf16(xr[:, :ROW_WORDS])
        xb_hi, xb_lo = _split_bf16(xr[:, ROW_WORDS:])
        x8 = jnp.concatenate([xa_hi, xb_hi, xa_lo, xb_lo,
                              jnp.zeros((4, ROW_WORDS), BF16)], axis=0)
        res = lax.dot_general(x8, g, (((1,), (1,)), ((), ())),
                              preferred_element_type=F32)
        odd = _lane_parity((1, 2 * N_PAIRS))
        part = jnp.where(odd, res[1:2] + res[3:4], res[0:1] + res[2:3])
        sc_ref[pl.ds(t, 1), :] = part + pltpu.roll(part, 1, axis=1)

    _pipelined_tokens(tb, tiles, scores, ids_ref, tab_ref)

    r = lax.broadcasted_iota(jnp.int32, (N_PAIRS, 2 * N_PAIRS), 0)
    c = lax.broadcasted_iota(jnp.int32, (N_PAIRS, 2 * N_PAIRS), 1)
    spread = jnp.where(c == 2 * r + 1, 1.0, 0.0).astype(BF16)
    g_hi, g_lo = _split_bf16(gates_ref[...])
    g2 = (jnp.dot(g_hi, spread, preferred_element_type=F32)
          + jnp.dot(g_lo, spread, preferred_element_type=F32))
    w_ref[...] = g2 * _gelu(sc_ref[...])


def _table_spec():
    return pl.BlockSpec((N_EXPERTS * SLAB, LANES), lambda i: (0, 0),
                        pipeline_mode=pl.Buffered(1))


def _tile_scratch(n):
    return [pltpu.VMEM((SLAB * TILE_STRIDE, LANES), jnp.int32) for _ in range(n)]


def peer_scores(ids, xn, gates, tab, tb=512):
    t = xn.shape[0]
    return pl.pallas_call(
        _peer_u_kernel,
        grid=(t // tb,),
        in_specs=[pl.BlockSpec((tb, N_PAIRS), lambda i: (i, 0), memory_space=pltpu.SMEM),
                  pl.BlockSpec((tb, D_MODEL), lambda i: (i, 0)),
                  pl.BlockSpec((tb, N_PAIRS), lambda i: (i, 0)),
                  _table_spec()],
        out_specs=pl.BlockSpec((tb, 2 * N_PAIRS), lambda i: (i, 0)),
        out_shape=jax.ShapeDtypeStruct((t, 2 * N_PAIRS), F32),
        scratch_shapes=([pltpu.VMEM((tb, 2 * N_PAIRS), F32)]
                        + _tile_scratch(SCORES_TOKENS_IN_FLIGHT)),
        compiler_params=_params(("arbitrary",), VMEM_LIMIT),
        name="peer_u",
    )(ids, xn, gates, tab)


def _peer_v_kernel(ids_ref, w_ref, h_ref, tab_ref, o_ref, acc_ref, lhs_refs, *tiles):
    tb = h_ref.shape[0]

    @pl.when(pl.program_id(0) == 0)
    def _():
        for tile in tiles:
            tile[...] = jnp.zeros_like(tile)

    w_odd = w_ref[...]
    w_even = pltpu.roll(w_odd, 2 * N_PAIRS - 1, axis=1)
    for i, w in enumerate((w_even, w_odd)):
        hi = w.astype(BF16).astype(F32)
        lhs_refs[i, :, :] = hi
        lhs_refs[2 + i, :, :] = w - hi

    def combine(t, tile):
        rid = lax.broadcasted_iota(jnp.int32, (8, 2 * N_PAIRS), 0)
        lhs = jnp.zeros((8, 2 * N_PAIRS), F32)
        for i in range(4):
            row = lhs_refs[i, pl.ds(t, 1), :]
            lhs = jnp.where(rid == i, row, lhs)
        lhs = lhs.astype(BF16)
        lo, hi = [], []
        for s in range(SLAB):
            res = jnp.dot(lhs, _tile_chunk(tile, s), preferred_element_type=F32)
            lo.append(res[0:1] + res[2:3])
            hi.append(res[1:2] + res[3:4])
        acc_ref[pl.ds(t, 1), :] = jnp.concatenate(lo + hi, axis=-1)

    _pipelined_tokens(tb, tiles, combine, ids_ref, tab_ref)
    o_ref[...] = h_ref[...] + acc_ref[...]


def peer_combine(ids, w, h, tab, tb=512):
    t = h.shape[0]
    return pl.pallas_call(
        _peer_v_kernel,
        grid=(t // tb,),
        in_specs=[pl.BlockSpec((tb, N_PAIRS), lambda i: (i, 0), memory_space=pltpu.SMEM),
                  pl.BlockSpec((tb, 2 * N_PAIRS), lambda i: (i, 0)),
                  pl.BlockSpec((tb, D_MODEL), lambda i: (i, 0)),
                  _table_spec()],
        out_specs=pl.BlockSpec((tb, D_MODEL), lambda i: (i, 0)),
        out_shape=jax.ShapeDtypeStruct((t, D_MODEL), F32),
        scratch_shapes=[pltpu.VMEM((tb, D_MODEL), F32),
                        pltpu.VMEM((4, tb, 2 * N_PAIRS), F32)]
                       + _tile_scratch(COMBINE_TOKENS_IN_FLIGHT),
        compiler_params=_params(("arbitrary",), VMEM_LIMIT),
        name="peer_v",
    )(ids, w, h, tab)


def _ple_kernel(h_ref, p_ref, g_ref, wg_ref, wp_ref, gf_ref, o_ref, *, final):
    h = h_ref[...]
    gate = _sigmoid(jnp.dot(_rms(h, g_ref[...]).astype(BF16), wg_ref[...],
                            preferred_element_type=F32))
    out = h + gate * jnp.dot(p_ref[...].astype(BF16), wp_ref[...],
                             preferred_element_type=F32)
    if final:
        out = _rms(out, gf_ref[...])
    o_ref[...] = out


def ple(h, p, layer, g, wg, wp, gf, final, tm=512):
    t = h.shape[0]
    tm = min(tm, t)
    return pl.pallas_call(
        functools.partial(_ple_kernel, final=final),
        grid=(t // tm,),
        in_specs=[pl.BlockSpec((tm, D_MODEL), lambda i: (i, 0)),
                  pl.BlockSpec((None, tm, PLE_DIM), lambda i: (layer, i, 0)),
                  pl.BlockSpec((1, D_MODEL), lambda i: (0, 0)),
                  pl.BlockSpec(wg.shape, lambda i: (0, 0)),
                  pl.BlockSpec(wp.shape, lambda i: (0, 0)),
                  pl.BlockSpec((1, D_MODEL), lambda i: (0, 0))],
        out_specs=pl.BlockSpec((tm, D_MODEL), lambda i: (i, 0)),
        out_shape=jax.ShapeDtypeStruct((t, D_MODEL), F32),
        compiler_params=_params(("arbitrary",), VMEM_LIMIT),
        name="ple",
    )(h, p, g, wg, wp, gf)


def _cast_kernel(w_ref, o_ref):
    o_ref[...] = w_ref[...].astype(o_ref.dtype)


def _permute_w_in(w_in, layer, tn=2 * KV_W):
    nb = IN_COLS // tn
    q_blocks, kv_blocks = D_MODEL // tn, 2 * KV_W // tn

    def src(j):
        return jnp.where(j < q_blocks, j,
                         jnp.where(j < nb - kv_blocks, j + kv_blocks, j - (nb - kv_blocks) + q_blocks))

    return pl.pallas_call(
        _cast_kernel,
        grid=(nb,),
        in_specs=[pl.BlockSpec((None, D_MODEL, tn), lambda j: (layer, 0, src(j)))],
        out_specs=pl.BlockSpec((D_MODEL, tn), lambda j: (0, j)),
        out_shape=jax.ShapeDtypeStruct((D_MODEL, IN_COLS), BF16),
        compiler_params=_params(("arbitrary",)),
        name="cast_w_in",
    )(w_in)


def _bf16_high_bits(x):
    bits = lax.bitcast_convert_type(x, jnp.uint32)
    return bits + (jnp.uint32(0x7FFF) + ((bits >> 16) & jnp.uint32(1)))


def _pack_kernel(t_ref, o_ref):
    x = t_ref[...]
    rows = x.shape[0]
    for s in range(SLAB):
        lo = _bf16_high_bits(x[:, s * LANES:(s + 1) * LANES])
        hi = _bf16_high_bits(x[:, ROW_WORDS + s * LANES:ROW_WORDS + (s + 1) * LANES])
        words = (lo >> 16) | (hi & jnp.uint32(0xFFFF0000))
        o_ref[pl.ds(s, rows, stride=SLAB), :] = lax.bitcast_convert_type(words, jnp.int32)


def _pack_table(tabs, layer, te=512):
    return pl.pallas_call(
        _pack_kernel,
        grid=(N_EXPERTS // te,),
        in_specs=[pl.BlockSpec((None, te, D_MODEL), lambda i: (layer, i, 0))],
        out_specs=pl.BlockSpec((te * SLAB, LANES), lambda i: (i, 0)),
        out_shape=jax.ShapeDtypeStruct((N_EXPERTS * SLAB, LANES), jnp.int32),
        compiler_params=_params(("arbitrary",)),
        name="pack_table",
    )(tabs)


def kernel(x, p, norm_mix, w_in, attn_sinks, conv_w, rec_conv_w, rec_conv_b, w_rgate, b_rgate, w_igate, b_igate, lru_lambda, w_branch, w_out, norm_ffn, w_peer_q, peer_sub_keys, peer_u, peer_v, norm_ple, w_ple_gate, w_ple_proj, norm_final):
    batch, seq, d = x.shape
    depth = w_in.shape[0]
    t = batch * seq
    row = lambda a: a.reshape(1, -1)
    h = x.reshape(t, d)
    p3 = p.reshape(depth, t, -1)
    for l in range(depth):
        z = norm_matmul(h, row(norm_mix[l]), _permute_w_in(w_in, l))
        attn = attention(z, attn_sinks[l], batch, seq)
        conv, rec = conv_rec(z, conv_w[l], rec_conv_w[l], row(rec_conv_b[l]),
                             w_rgate[l].astype(BF16), w_igate[l].astype(BF16),
                             row(b_rgate[l]), row(b_igate[l]), row(lru_lambda[l]), batch, seq)
        h = merge(attn, conv, rec, z, h, w_branch[l].astype(BF16), w_out[l].astype(BF16))
        xn, ids, gates = route(h, row(norm_ffn[l]), w_peer_q[l].astype(BF16), peer_sub_keys[l])
        w = peer_scores(ids, xn, gates, _pack_table(peer_u, l))
        h = peer_combine(ids, w, h, _pack_table(peer_v, l))
        h = ple(h, p3, l, row(norm_ple[l]), w_ple_gate[l].astype(BF16),
                w_ple_proj[l].astype(BF16), row(norm_final), final=(l == depth - 1))
    return h.reshape(batch, seq, d)
```

```python
import functools

import jax
import jax.numpy as jnp
from jax import lax
from jax.experimental import pallas as pl
from jax.experimental.pallas import tpu as pltpu

F32 = jnp.float32
BF16 = jnp.bfloat16

D_MODEL = 1024
N_Q_HEADS = 16
N_KV_HEADS = 4
HEAD_DIM = 64
GROUP = N_Q_HEADS // N_KV_HEADS
ATTN_BLOCK = 128
ATTN_BLOCKS_PER_STEP = 2
KV_W = N_KV_HEADS * HEAD_DIM
REC_HEADS = 4
REC_HEAD_DIM = D_MODEL // REC_HEADS
LRU_C = 8.0
N_KEYS = 128
N_EXPERTS = N_KEYS * N_KEYS
PEER_HEADS = 8
PEER_TOPK = 16
PEER_HALF = 128
N_PAIRS = PEER_HEADS * PEER_TOPK
PLE_DIM = 256
EPS = 1e-6
NEG_INF = -1e30
TAG_NONE = float(1 << 20)
ROUTE_HEADS_PER_STEP = 8

COL_Q, COL_CB, COL_CC, COL_CX, COL_RX, COL_RY, COL_G0 = 0, 1, 2, 3, 4, 5, 6
IN_COLS = 9 * D_MODEL + 2 * KV_W
COL_K_KV = 9 * D_MODEL // KV_W
COL_V_KV = COL_K_KV + 1

LANES = 128
SLAB = D_MODEL // 2 // LANES
ROW_WORDS = SLAB * LANES
TILE_STRIDE = N_PAIRS + 8
INDEX_COLS = 8
SCORES_TOKENS_IN_FLIGHT = 16
COMBINE_TOKENS_IN_FLIGHT = 16
VMEM_LIMIT = 56 * 1024 * 1024


def _params(sem, vmem=None):
    return pltpu.CompilerParams(dimension_semantics=sem, vmem_limit_bytes=vmem)


def _rms(x32, g):
    ms = jnp.mean(x32 * x32, axis=-1, keepdims=True)
    return x32 * lax.rsqrt(ms + EPS) * g


def _sigmoid(x):
    return 0.5 * jnp.tanh(0.5 * x) + 0.5


def _gelu(x):
    return 0.5 * x * (1.0 + jnp.tanh(0.7978845608028654 * (x + 0.044715 * (x * x * x))))


def _split_bf16(x32):
    hi = x32.astype(BF16)
    lo = (x32 - hi.astype(F32)).astype(BF16)
    return hi, lo


def _norm_matmul_kernel(x_ref, g_ref, w_ref, o_ref, xn_ref):
    @pl.when(pl.program_id(1) == 0)
    def _():
        xn_ref[...] = _rms(x_ref[...], g_ref[...]).astype(BF16)

    o_ref[...] = jnp.dot(xn_ref[...], w_ref[...],
                         preferred_element_type=F32).astype(o_ref.dtype)


def norm_matmul(x, g, w, tm=2048, tn=1024):
    m, d = x.shape
    n = w.shape[1]
    tm = min(tm, m)
    return pl.pallas_call(
        _norm_matmul_kernel,
        grid=(m // tm, pl.cdiv(n, tn)),
        in_specs=[pl.BlockSpec((tm, d), lambda i, j: (i, 0)),
                  pl.BlockSpec((1, d), lambda i, j: (0, 0)),
                  pl.BlockSpec((d, tn), lambda i, j: (0, j))],
        out_specs=pl.BlockSpec((tm, tn), lambda i, j: (i, j)),
        out_shape=jax.ShapeDtypeStruct((m, n), BF16),
        scratch_shapes=[pltpu.VMEM((tm, d), BF16)],
        compiler_params=_params(("arbitrary", "arbitrary"), VMEM_LIMIT),
        name="norm_matmul",
    )(x, g, w)


def _attn_block(sink_ref, q, k, v, first):
    shape = (2 * ATTN_BLOCK, GROUP * ATTN_BLOCK)
    ki = lax.broadcasted_iota(jnp.int32, shape, 0)
    col = lax.broadcasted_iota(jnp.int32, shape, 1)
    diff = (col & (ATTN_BLOCK - 1)) + ATTN_BLOCK - ki
    inside = jnp.where(diff >= 0, jnp.where(diff < ATTN_BLOCK, ki - first, -1), -1)
    mask = inside >= 0
    grp = lax.broadcasted_iota(jnp.int32, (1, GROUP * ATTN_BLOCK), 1) // ATTN_BLOCK
    ones = jnp.ones((2 * ATTN_BLOCK, HEAD_DIM), BF16)
    scores, sinks = [], []
    for h in range(N_KV_HEADS):
        kh = k[:, h * HEAD_DIM:(h + 1) * HEAD_DIM]
        qh = jnp.concatenate([q[:, (h * GROUP + g) * HEAD_DIM:(h * GROUP + g + 1) * HEAD_DIM]
                              for g in range(GROUP)], axis=0)
        sink = jnp.zeros((1, GROUP * ATTN_BLOCK), F32)
        for g in range(GROUP):
            sink = jnp.where(grp == g, sink_ref[h * GROUP + g], sink)
        sinks.append(sink)
        scores.append(lax.dot_general(kh, qh, (((1,), (1,)), ((), ())),
                                      preferred_element_type=F32))
    probs, sink_terms = [], []
    for h in range(N_KV_HEADS):
        s = jnp.where(mask, scores[h], NEG_INF)
        m = jnp.maximum(jnp.max(s, axis=0, keepdims=True), sinks[h])
        probs.append(jnp.exp(s - m).astype(BF16))
        sink_terms.append(jnp.exp(sinks[h] - m))
    outs = []
    for h in range(N_KV_HEADS):
        vh = jnp.concatenate([v[:, h * HEAD_DIM:(h + 1) * HEAD_DIM], ones], axis=-1)
        ov = lax.dot_general(vh, probs[h], (((0,), (0,)), ((), ())),
                             preferred_element_type=F32)
        denom = ov[HEAD_DIM:HEAD_DIM + 1, :] + sink_terms[h]
        o = (ov[:HEAD_DIM, :] / denom).T
        outs.extend(o[g * ATTN_BLOCK:(g + 1) * ATTN_BLOCK, :] for g in range(GROUP))
    return jnp.concatenate(outs, axis=-1)


def _attn_kernel(sink_ref, q_ref, kc_ref, vc_ref, kp_ref, vp_ref, o_ref):
    n = pl.program_id(1)
    k = jnp.concatenate([kp_ref[...], kc_ref[...]], axis=0)
    v = jnp.concatenate([vp_ref[...], vc_ref[...]], axis=0)
    for sub in range(ATTN_BLOCKS_PER_STEP):
        rows = slice(sub * ATTN_BLOCK, (sub + 1) * ATTN_BLOCK)
        keys = slice(sub * ATTN_BLOCK, (sub + 2) * ATTN_BLOCK)
        q = q_ref[rows, :] * (HEAD_DIM ** -0.5)
        first = jnp.where(n > 0, 0, ATTN_BLOCK) if sub == 0 else 0
        o_ref[rows, :] = _attn_block(sink_ref, q, k[keys, :], v[keys, :], first).astype(o_ref.dtype)


def attention(z, sinks, batch, seq):
    nb = seq // ATTN_BLOCK
    nsteps = nb // ATTN_BLOCKS_PER_STEP
    rows = ATTN_BLOCKS_PER_STEP * ATTN_BLOCK
    t = batch * seq
    cur = lambda b, n: b * nsteps + n
    prev = lambda b, n: b * nb + jnp.maximum(ATTN_BLOCKS_PER_STEP * n - 1, 0)
    return pl.pallas_call(
        _attn_kernel,
        grid=(batch, nsteps),
        in_specs=[pl.BlockSpec(memory_space=pltpu.SMEM),
                  pl.BlockSpec((rows, D_MODEL), lambda b, n: (cur(b, n), COL_Q)),
                  pl.BlockSpec((rows, KV_W), lambda b, n: (cur(b, n), COL_K_KV)),
                  pl.BlockSpec((rows, KV_W), lambda b, n: (cur(b, n), COL_V_KV)),
                  pl.BlockSpec((ATTN_BLOCK, KV_W), lambda b, n: (prev(b, n), COL_K_KV)),
                  pl.BlockSpec((ATTN_BLOCK, KV_W), lambda b, n: (prev(b, n), COL_V_KV))],
        out_specs=pl.BlockSpec((rows, D_MODEL), lambda b, n: (cur(b, n), 0)),
        out_shape=jax.ShapeDtypeStruct((t, D_MODEL), BF16),
        compiler_params=_params(("arbitrary", "arbitrary")),
        name="swa_attention",
    )(sinks, z, z, z, z, z)


HALO = 8


def _convrec_kernel(cb_ref, cc_ref, cx_ref, rx_ref, ry_ref, cw_ref, rcw_ref, rcb_ref,
                    wr_ref, wi_ref, br_ref, bi_ref, lam_ref, conv_ref, rec_ref,
                    ubuf, xbuf, hcar):
    ts = cb_ref.shape[0]

    @pl.when(pl.program_id(1) == 0)
    def _():
        ubuf[0:HALO, :] = jnp.zeros((HALO, D_MODEL), F32)
        xbuf[0:HALO, :] = jnp.zeros((HALO, D_MODEL), F32)
        hcar[...] = jnp.zeros_like(hcar)

    u = cc_ref[...].astype(F32) * cx_ref[...].astype(F32)
    ubuf[HALO:HALO + ts, :] = u
    y = (cw_ref[2:3, :] * u + cw_ref[1:2, :] * ubuf[HALO - 1:HALO - 1 + ts, :]
         + cw_ref[0:1, :] * ubuf[HALO - 2:HALO - 2 + ts, :])
    conv_ref[...] = (cb_ref[...].astype(F32) * y).astype(conv_ref.dtype)
    ubuf[0:HALO, :] = ubuf[ts:ts + HALO, :]

    x = rx_ref[...].astype(F32)
    xbuf[HALO:HALO + ts, :] = x
    c = (rcw_ref[3:4, :] * x + rcw_ref[2:3, :] * xbuf[HALO - 1:HALO - 1 + ts, :]
         + rcw_ref[1:2, :] * xbuf[HALO - 2:HALO - 2 + ts, :]
         + rcw_ref[0:1, :] * xbuf[HALO - 3:HALO - 3 + ts, :] + rcb_ref[...])
    xbuf[0:HALO, :] = xbuf[ts:ts + HALO, :]
    cbf = c.astype(BF16)
    r_parts, i_parts = [], []
    for h in range(REC_HEADS):
        ch = cbf[:, h * REC_HEAD_DIM:(h + 1) * REC_HEAD_DIM]
        r_parts.append(jnp.dot(ch, wr_ref[h], preferred_element_type=F32))
        i_parts.append(jnp.dot(ch, wi_ref[h], preferred_element_type=F32))
    r = _sigmoid(jnp.concatenate(r_parts, axis=-1) + br_ref[...])
    ig = _sigmoid(jnp.concatenate(i_parts, axis=-1) + bi_ref[...])
    nl = -lam_ref[...]
    softplus = jnp.maximum(nl, 0.0) + jnp.log(1.0 + jnp.exp(-jnp.abs(nl)))
    log_a = -LRU_C * r * softplus
    a = jnp.exp(log_a)
    b = jnp.sqrt(1.0 - jnp.exp(2.0 * log_a)) * (ig * c)
    a3 = a.reshape(ts // 8, 8, D_MODEL)
    b3 = b.reshape(ts // 8, 8, D_MODEL)
    rid = lax.broadcasted_iota(jnp.int32, a3.shape, 1)
    for d in (1, 2, 4):
        inside = rid >= d
        a_sh = jnp.where(inside, pltpu.roll(a3, d, axis=1), 1.0)
        b_sh = jnp.where(inside, pltpu.roll(b3, d, axis=1), 0.0)
        b3 = a3 * b_sh + b3
        a3 = a3 * a_sh
    carry = hcar[0:1, :]
    groups = []
    for g in range(ts // 8):
        hg = b3[g] + a3[g] * carry
        groups.append(hg)
        carry = hg[7:8, :]
    hcar[0:1, :] = carry
    hs = jnp.concatenate(groups, axis=0)
    rec_ref[...] = (_gelu(ry_ref[...].astype(F32)) * hs).astype(rec_ref.dtype)


def conv_rec(z, cw, rcw, rcb, wr, wi, br, bi, lam, batch, seq, ts=256):
    ts = min(ts, seq)
    nt = seq // ts
    t = batch * seq
    row = lambda b, i: b * nt + i
    zspec = lambda col: pl.BlockSpec((ts, D_MODEL), lambda b, i: (row(b, i), col))
    full = lambda a: pl.BlockSpec(a.shape, lambda b, i: (0,) * a.ndim)
    return pl.pallas_call(
        _convrec_kernel,
        grid=(batch, nt),
        in_specs=[zspec(COL_CB), zspec(COL_CC), zspec(COL_CX), zspec(COL_RX), zspec(COL_RY),
                  full(cw), full(rcw), full(rcb), full(wr), full(wi), full(br), full(bi),
                  full(lam)],
        out_specs=[pl.BlockSpec((ts, D_MODEL), lambda b, i: (row(b, i), 0)),
                   pl.BlockSpec((ts, D_MODEL), lambda b, i: (row(b, i), 0))],
        out_shape=[jax.ShapeDtypeStruct((t, D_MODEL), BF16),
                   jax.ShapeDtypeStruct((t, D_MODEL), BF16)],
        scratch_shapes=[pltpu.VMEM((ts + HALO, D_MODEL), F32),
                        pltpu.VMEM((ts + HALO, D_MODEL), F32),
                        pltpu.VMEM((HALO, D_MODEL), F32)],
        compiler_params=_params(("arbitrary", "arbitrary"), VMEM_LIMIT),
        name="conv_rec",
    )(z, z, z, z, z, cw, rcw, rcb, wr, wi, br, bi, lam)


def _merge_kernel(a_ref, c_ref, r_ref, g0_ref, g1_ref, g2_ref, h_ref, wb_ref, wo_ref, o_ref):
    merged = None
    for n, (br, gt) in enumerate(((a_ref, g0_ref), (c_ref, g1_ref), (r_ref, g2_ref))):
        y = jnp.dot(br[...], wb_ref[n], preferred_element_type=F32)
        term = _sigmoid(gt[...].astype(F32)) * y
        merged = term if merged is None else merged + term
    o_ref[...] = h_ref[...] + jnp.dot(merged.astype(BF16), wo_ref[...],
                                      preferred_element_type=F32)


def merge(attn, conv, rec, z, h, wb, wo, tm=512):
    t = h.shape[0]
    tm = min(tm, t)
    blk = lambda col: pl.BlockSpec((tm, D_MODEL), lambda i: (i, col))
    return pl.pallas_call(
        _merge_kernel,
        grid=(t // tm,),
        in_specs=[blk(0), blk(0), blk(0), blk(COL_G0), blk(COL_G0 + 1), blk(COL_G0 + 2), blk(0),
                  pl.BlockSpec(wb.shape, lambda i: (0, 0, 0)),
                  pl.BlockSpec(wo.shape, lambda i: (0, 0))],
        out_specs=blk(0),
        out_shape=jax.ShapeDtypeStruct((t, D_MODEL), F32),
        compiler_params=_params(("arbitrary",), VMEM_LIMIT),
        name="merge",
    )(attn, conv, rec, z, z, z, h, wb, wo)


def _sorting_network(n):
    pairs = []
    p = 1
    while p < n:
        k = p
        while k >= 1:
            for j in range(k % p, n - k, 2 * k):
                for i in range(min(k, n - j - k)):
                    if (i + j) // (2 * p) == (i + j + k) // (2 * p):
                        pairs.append((i + j, i + j + k))
            k //= 2
        p *= 2
    return pairs


def _topk_columns(vals, tags, k):
    vals, tags = list(vals), list(tags)
    n = len(vals)
    for i, j in _sorting_network(n):
        va, ta, vb, tb = vals[i], tags[i], vals[j], tags[j]
        first = (va > vb) | ((va == vb) & (ta < tb))
        vals[i], vals[j] = jnp.where(first, va, vb), jnp.where(first, vb, va)
        tags[i], tags[j] = jnp.where(first, ta, tb), jnp.where(first, tb, ta)
    out_v, out_t = [], []
    for it in range(k):
        m = jnp.max(vals[0], axis=0, keepdims=True)
        best = jnp.min(jnp.where(vals[0] == m, tags[0], TAG_NONE), axis=0, keepdims=True)
        out_v.append(m)
        out_t.append(best)
        left = k - 1 - it
        pop = tags[0] == best
        for d in range(min(left, n - 1)):
            vals[d] = jnp.where(pop, vals[d + 1], vals[d])
            tags[d] = jnp.where(pop, tags[d + 1], tags[d])
        if left > n - 1:
            vals[n - 1] = jnp.where(pop, -jnp.inf, vals[n - 1])
            tags[n - 1] = jnp.where(pop, TAG_NONE, tags[n - 1])
    return out_v, out_t


def _stack_rows(rows, lo, hi):
    n = hi - lo
    rid = lax.broadcasted_iota(jnp.int32, (n, rows[0].shape[1]), 0)
    out = jnp.broadcast_to(rows[lo], (n, rows[0].shape[1]))
    for i in range(1, n):
        out = jnp.where(rid == i, rows[lo + i], out)
    return out


def _pair_candidates(s1, s2):
    k = PEER_TOPK
    tm = s1[0].shape[1]
    r = lax.broadcasted_iota(jnp.int32, (8, tm), 0)
    a_lo, a_hi = _stack_rows(s1, 0, 8), _stack_rows(s1, 8, 16)
    b_lo, b_hi = _stack_rows(s2, 0, 8), _stack_rows(s2, 8, 16)

    def slab(vals, tag, lo, hi):
        keep = jnp.where(r >= lo, r, hi + 1) <= hi
        return jnp.where(keep, vals, -jnp.inf), tag.astype(F32)

    slabs = [slab(a_lo + s2[0], r * k, 0, 7),
             slab(a_hi + s2[0], (r + 8) * k, 0, 7),
             slab(s1[0] + b_lo, r, 1, 7),
             slab(s1[0] + b_hi, r + 8, 0, 7),
             slab(a_lo + s2[1], r * k + 1, 1, 7),
             slab(s1[1] + b_lo, r + k, 2, 7),
             slab(a_lo + s2[2], r * k + 2, 2, 4),
             slab(a_lo + s2[3], r * k + 3, 2, 3),
             slab(a_lo + s2[4], r * k + 4, 2, 2)]
    return [v for v, _ in slabs], [t for _, t in slabs]


def _route_kernel(h_ref, g_ref, wq_ref, keys_ref, xn_ref, ids_ref, gates_ref,
                  q_ref, ids_t, gates_t):
    xn = _rms(h_ref[...], g_ref[...])
    xn_ref[...] = xn
    q_ref[...] = jnp.dot(xn.astype(BF16), wq_ref[...], preferred_element_type=F32)
    tm = h_ref.shape[0]
    row8 = lax.broadcasted_iota(jnp.int32, (8, tm), 0)
    key_tags = [(row8 + 8 * g).astype(F32) for g in range(N_KEYS // 8)]
    key_parts = [_split_bf16(keys_ref[p]) for p in range(2)]

    def one_head(hd):
        tops = []
        for p in range(2):
            c0 = pl.multiple_of((hd * 2 + p) * PEER_HALF, PEER_HALF)
            q_hi, q_lo = _split_bf16(q_ref[:, pl.ds(c0, PEER_HALF)])
            k_hi, k_lo = key_parts[p]
            nt = (((1,), (1,)), ((), ()))
            sc = (lax.dot_general(k_hi, q_hi, nt, preferred_element_type=F32)
                  + lax.dot_general(k_hi, q_lo, nt, preferred_element_type=F32)
                  + lax.dot_general(k_lo, q_hi, nt, preferred_element_type=F32))
            slabs = [sc[8 * g:8 * g + 8, :] for g in range(N_KEYS // 8)]
            tops.append(_topk_columns(slabs, key_tags, PEER_TOPK))
        (s1, i1), (s2, i2) = tops
        cand, flat = _pair_candidates(s1, s2)
        best_s, best_c = _topk_columns(cand, flat, PEER_TOPK)
        best_s = _stack_rows(best_s, 0, PEER_TOPK)
        flat_c = _stack_rows(best_c, 0, PEER_TOPK).astype(jnp.int32)
        a_idx = flat_c >> 4
        b_idx = flat_c & (PEER_TOPK - 1)
        e1 = jnp.zeros(flat_c.shape, F32)
        e2 = jnp.zeros(flat_c.shape, F32)
        for a in range(PEER_TOPK):
            e1 = jnp.where(a_idx == a, i1[a], e1)
            e2 = jnp.where(b_idx == a, i2[a], e2)
        ex = jnp.exp(best_s - best_s[0:1, :])
        r0 = pl.multiple_of(hd * PEER_TOPK, PEER_TOPK)
        gates_t[pl.ds(r0, PEER_TOPK), :] = ex / jnp.sum(ex, axis=0, keepdims=True)
        ids_t[pl.ds(r0, PEER_TOPK), :] = ((e1 * N_KEYS + e2) * SLAB).astype(jnp.int32)

    def heads(step, carry):
        for i in range(ROUTE_HEADS_PER_STEP):
            one_head(step * ROUTE_HEADS_PER_STEP + i)
        return carry

    lax.fori_loop(0, PEER_HEADS // ROUTE_HEADS_PER_STEP, heads, 0)
    ids_ref[...] = ids_t[...].T
    gates_ref[...] = gates_t[...].T


def route(h, g, wq, keys, tm=128):
    t = h.shape[0]
    return pl.pallas_call(
        _route_kernel,
        grid=(t // tm,),
        in_specs=[pl.BlockSpec((tm, D_MODEL), lambda i: (i, 0)),
                  pl.BlockSpec((1, D_MODEL), lambda i: (0, 0)),
                  pl.BlockSpec(wq.shape, lambda i: (0, 0)),
                  pl.BlockSpec(keys.shape, lambda i: (0, 0, 0))],
        out_specs=[pl.BlockSpec((tm, D_MODEL), lambda i: (i, 0)),
                   pl.BlockSpec((tm, N_PAIRS), lambda i: (i, 0)),
                   pl.BlockSpec((tm, N_PAIRS), lambda i: (i, 0))],
        out_shape=[jax.ShapeDtypeStruct((t, D_MODEL), F32),
                   jax.ShapeDtypeStruct((t, N_PAIRS), jnp.int32),
                   jax.ShapeDtypeStruct((t, N_PAIRS), F32)],
        scratch_shapes=[pltpu.VMEM((tm, 2 * PEER_HALF * PEER_HEADS), F32),
                        pltpu.VMEM((N_PAIRS, tm), jnp.int32),
                        pltpu.VMEM((N_PAIRS, tm), F32)],
        compiler_params=_params(("arbitrary",), VMEM_LIMIT),
        name="peer_route",
    )(h, g, wq, keys)


def _gather_tokens(ids_ref, tab_ref, tiles, t0):
    zero = lax.min(t0, 0)
    offs = [zero + u for u in range(INDEX_COLS)]
    for jj in range(N_PAIRS // INDEX_COLS):
        rows = [ids_ref.at[t0 + q, pl.ds(INDEX_COLS * jj, INDEX_COLS)] for q in range(len(tiles))]
        for u in range(INDEX_COLS):
            j = INDEX_COLS * jj + u
            for q, tile in enumerate(tiles):
                e = pl.multiple_of(rows[q][offs[u]], SLAB)
                tile[pl.ds(j, SLAB, stride=TILE_STRIDE), :] = tab_ref[pl.ds(e, SLAB), :]


def _pipelined_tokens(tb, tiles, consume, ids_ref, tab_ref):
    nfl = len(tiles)

    def body(k, carry):
        t0 = nfl * k
        for q in range(nfl):
            consume(jnp.maximum(t0 - nfl + q, 0), tiles[q])
        _gather_tokens(ids_ref, tab_ref, tiles, t0)
        return carry

    lax.fori_loop(0, tb // nfl, body, 0)
    for q in range(nfl):
        consume(tb - nfl + q, tiles[q])


def _tile_chunk(tile, s):
    return pltpu.bitcast(tile[pl.ds(s * TILE_STRIDE, N_PAIRS), :], BF16)


def _lane_parity(shape):
    lane = lax.broadcasted_iota(jnp.int32, shape, len(shape) - 1)
    return (lane & 1) == 1


def _peer_u_kernel(ids_ref, x_ref, gates_ref, tab_ref, w_ref, sc_ref, *tiles):
    tb = x_ref.shape[0]

    @pl.when(pl.program_id(0) == 0)
    def _():
        for tile in tiles:
            tile[...] = jnp.zeros_like(tile)

    def scores(t, tile):
        g = jnp.concatenate([_tile_chunk(tile, s) for s in range(SLAB)], axis=-1)
        xr = x_ref[pl.ds(t, 1), :]
        xa_hi, xa_lo = _split_bf16(xr[:, :ROW_WORDS])
        xb_hi, xb_lo = _split_bf16(xr[:, ROW_WORDS:])
        x8 = jnp.concatenate([xa_hi, xb_hi, xa_lo, xb_lo,
                              jnp.zeros((4, ROW_WORDS), BF16)], axis=0)
        res = lax.dot_general(x8, g, (((1,), (1,)), ((), ())),
                              preferred_element_type=F32)
        odd = _lane_parity((1, 2 * N_PAIRS))
        part = jnp.where(odd, res[1:2] + res[3:4], res[0:1] + res[2:3])
        sc_ref[pl.ds(t, 1), :] = part + pltpu.roll(part, 1, axis=1)

    _pipelined_tokens(tb, tiles, scores, ids_ref, tab_ref)

    r = lax.broadcasted_iota(jnp.int32, (N_PAIRS, 2 * N_PAIRS), 0)
    c = lax.broadcasted_iota(jnp.int32, (N_PAIRS, 2 * N_PAIRS), 1)
    spread = jnp.where(c == 2 * r + 1, 1.0, 0.0).astype(BF16)
    g_hi, g_lo = _split_bf16(gates_ref[...])
    g2 = (jnp.dot(g_hi, spread, preferred_element_type=F32)
          + jnp.dot(g_lo, spread, preferred_element_type=F32))
    w_ref[...] = g2 * _gelu(sc_ref[...])


def _table_spec():
    return pl.BlockSpec((N_EXPERTS * SLAB, LANES), lambda i: (0, 0),
                        pipeline_mode=pl.Buffered(1))


def _tile_scratch(n):
    return [pltpu.VMEM((SLAB * TILE_STRIDE, LANES), jnp.int32) for _ in range(n)]


def peer_scores(ids, xn, gates, tab, tb=512):
    t = xn.shape[0]
    return pl.pallas_call(
        _peer_u_kernel,
        grid=(t // tb,),
        in_specs=[pl.BlockSpec((tb, N_PAIRS), lambda i: (i, 0), memory_space=pltpu.SMEM),
                  pl.BlockSpec((tb, D_MODEL), lambda i: (i, 0)),
                  pl.BlockSpec((tb, N_PAIRS), lambda i: (i, 0)),
                  _table_spec()],
        out_specs=pl.BlockSpec((tb, 2 * N_PAIRS), lambda i: (i, 0)),
        out_shape=jax.ShapeDtypeStruct((t, 2 * N_PAIRS), F32),
        scratch_shapes=([pltpu.VMEM((tb, 2 * N_PAIRS), F32)]
                        + _tile_scratch(SCORES_TOKENS_IN_FLIGHT)),
        compiler_params=_params(("arbitrary",), VMEM_LIMIT),
        name="peer_u",
    )(ids, xn, gates, tab)


def _peer_v_kernel(ids_ref, w_ref, h_ref, tab_ref, o_ref, acc_ref, lhs_refs, *tiles):
    tb = h_ref.shape[0]

    @pl.when(pl.program_id(0) == 0)
    def _():
        for tile in tiles:
            tile[...] = jnp.zeros_like(tile)

    w_odd = w_ref[...]
    w_even = pltpu.roll(w_odd, 2 * N_PAIRS - 1, axis=1)
    for i, w in enumerate((w_even, w_odd)):
        hi = w.astype(BF16).astype(F32)
        lhs_refs[i, :, :] = hi
        lhs_refs[2 + i, :, :] = w - hi

    def combine(t, tile):
        rid = lax.broadcasted_iota(jnp.int32, (8, 2 * N_PAIRS), 0)
        lhs = jnp.zeros((8, 2 * N_PAIRS), F32)
        for i in range(4):
            row = lhs_refs[i, pl.ds(t, 1), :]
            lhs = jnp.where(rid == i, row, lhs)
        lhs = lhs.astype(BF16)
        lo, hi = [], []
        for s in range(SLAB):
            res = jnp.dot(lhs, _tile_chunk(tile, s), preferred_element_type=F32)
            lo.append(res[0:1] + res[2:3])
            hi.append(res[1:2] + res[3:4])
        acc_ref[pl.ds(t, 1), :] = jnp.concatenate(lo + hi, axis=-1)

    _pipelined_tokens(tb, tiles, combine, ids_ref, tab_ref)
    o_ref[...] = h_ref[...] + acc_ref[...]


def peer_combine(ids, w, h, tab, tb=512):
    t = h.shape[0]
    return pl.pallas_call(
        _peer_v_kernel,
        grid=(t // tb,),
        in_specs=[pl.BlockSpec((tb, N_PAIRS), lambda i: (i, 0), memory_space=pltpu.SMEM),
                  pl.BlockSpec((tb, 2 * N_PAIRS), lambda i: (i, 0)),
                  pl.BlockSpec((tb, D_MODEL), lambda i: (i, 0)),
                  _table_spec()],
        out_specs=pl.BlockSpec((tb, D_MODEL), lambda i: (i, 0)),
        out_shape=jax.ShapeDtypeStruct((t, D_MODEL), F32),
        scratch_shapes=[pltpu.VMEM((tb, D_MODEL), F32),
                        pltpu.VMEM((4, tb, 2 * N_PAIRS), F32)]
                       + _tile_scratch(COMBINE_TOKENS_IN_FLIGHT),
        compiler_params=_params(("arbitrary",), VMEM_LIMIT),
        name="peer_v",
    )(ids, w, h, tab)


def _ple_kernel(h_ref, p_ref, g_ref, wg_ref, wp_ref, gf_ref, o_ref, *, final):
    h = h_ref[...]
    gate = _sigmoid(jnp.dot(_rms(h, g_ref[...]).astype(BF16), wg_ref[...],
                            preferred_element_type=F32))
    out = h + gate * jnp.dot(p_ref[...].astype(BF16), wp_ref[...],
                             preferred_element_type=F32)
    if final:
        out = _rms(out, gf_ref[...])
    o_ref[...] = out


def ple(h, p, layer, g, wg, wp, gf, final, tm=512):
    t = h.shape[0]
    tm = min(tm, t)
    return pl.pallas_call(
        functools.partial(_ple_kernel, final=final),
        grid=(t // tm,),
        in_specs=[pl.BlockSpec((tm, D_MODEL), lambda i: (i, 0)),
                  pl.BlockSpec((None, tm, PLE_DIM), lambda i: (layer, i, 0)),
                  pl.BlockSpec((1, D_MODEL), lambda i: (0, 0)),
                  pl.BlockSpec(wg.shape, lambda i: (0, 0)),
                  pl.BlockSpec(wp.shape, lambda i: (0, 0)),
                  pl.BlockSpec((1, D_MODEL), lambda i: (0, 0))],
        out_specs=pl.BlockSpec((tm, D_MODEL), lambda i: (i, 0)),
        out_shape=jax.ShapeDtypeStruct((t, D_MODEL), F32),
        compiler_params=_params(("arbitrary",), VMEM_LIMIT),
        name="ple",
    )(h, p, g, wg, wp, gf)


def _cast_kernel(w_ref, o_ref):
    o_ref[...] = w_ref[...].astype(o_ref.dtype)


def _permute_w_in(w_in, layer, tn=2 * KV_W):
    nb = IN_COLS // tn
    q_blocks, kv_blocks = D_MODEL // tn, 2 * KV_W // tn

    def src(j):
        return jnp.where(j < q_blocks, j,
                         jnp.where(j < nb - kv_blocks, j + kv_blocks, j - (nb - kv_blocks) + q_blocks))

    return pl.pallas_call(
        _cast_kernel,
        grid=(nb,),
        in_specs=[pl.BlockSpec((None, D_MODEL, tn), lambda j: (layer, 0, src(j)))],
        out_specs=pl.BlockSpec((D_MODEL, tn), lambda j: (0, j)),
        out_shape=jax.ShapeDtypeStruct((D_MODEL, IN_COLS), BF16),
        compiler_params=_params(("arbitrary",)),
        name="cast_w_in",
    )(w_in)


def _bf16_high_bits(x):
    bits = lax.bitcast_convert_type(x, jnp.uint32)
    return bits + (jnp.uint32(0x7FFF) + ((bits >> 16) & jnp.uint32(1)))


def _pack_kernel(t_ref, o_ref):
    x = t_ref[...]
    rows = x.shape[0]
    for s in range(SLAB):
        lo = _bf16_high_bits(x[:, s * LANES:(s + 1) * LANES])
        hi = _bf16_high_bits(x[:, ROW_WORDS + s * LANES:ROW_WORDS + (s + 1) * LANES])
        words = (lo >> 16) | (hi & jnp.uint32(0xFFFF0000))
        o_ref[pl.ds(s, rows, stride=SLAB), :] = lax.bitcast_convert_type(words, jnp.int32)


def _pack_table(tabs, layer, te=512):
    return pl.pallas_call(
        _pack_kernel,
        grid=(N_EXPERTS // te,),
        in_specs=[pl.BlockSpec((None, te, D_MODEL), lambda i: (layer, i, 0))],
        out_specs=pl.BlockSpec((te * SLAB, LANES), lambda i: (i, 0)),
        out_shape=jax.ShapeDtypeStruct((N_EXPERTS * SLAB, LANES), jnp.int32),
        compiler_params=_params(("arbitrary",)),
        name="pack_table",
    )(tabs)


def kernel(x, p, norm_mix, w_in, attn_sinks, conv_w, rec_conv_w, rec_conv_b, w_rgate, b_rgate, w_igate, b_igate, lru_lambda, w_branch, w_out, norm_ffn, w_peer_q, peer_sub_keys, peer_u, peer_v, norm_ple, w_ple_gate, w_ple_proj, norm_final):
    batch, seq, d = x.shape
    depth = w_in.shape[0]
    t = batch * seq
    row = lambda a: a.reshape(1, -1)
    h = x.reshape(t, d)
    p3 = p.reshape(depth, t, -1)
    for l in range(depth):
        z = norm_matmul(h, row(norm_mix[l]), _permute_w_in(w_in, l))
        attn = attention(z, attn_sinks[l], batch, seq)
        conv, rec = conv_rec(z, conv_w[l], rec_conv_w[l], row(rec_conv_b[l]),
                             w_rgate[l].astype(BF16), w_igate[l].astype(BF16),
                             row(b_rgate[l]), row(b_igate[l]), row(lru_lambda[l]), batch, seq)
        h = merge(attn, conv, rec, z, h, w_branch[l].astype(BF16), w_out[l].astype(BF16))
        xn, ids, gates = route(h, row(norm_ffn[l]), w_peer_q[l].astype(BF16), peer_sub_keys[l])
        w = peer_scores(ids, xn, gates, _pack_table(peer_u, l))
        h = peer_combine(ids, w, h, _pack_table(peer_v, l))
        h = ple(h, p3, l, row(norm_ple[l]), w_ple_gate[l].astype(BF16),
                w_ple_proj[l].astype(BF16), row(norm_final), final=(l == depth - 1))
    return h.reshape(batch, seq, d)
```

```python
import functools

import jax
import jax.numpy as jnp
from jax import lax
from jax.experimental import pallas as pl
from jax.experimental.pallas import tpu as pltpu

F32 = jnp.float32
BF16 = jnp.bfloat16

D_MODEL = 1024
N_Q_HEADS = 16
N_KV_HEADS = 4
HEAD_DIM = 64
GROUP = N_Q_HEADS // N_KV_HEADS
ATTN_BLOCK = 128
ATTN_BLOCKS_PER_STEP = 4
KV_W = N_KV_HEADS * HEAD_DIM
REC_HEADS = 4
REC_HEAD_DIM = D_MODEL // REC_HEADS
LRU_C = 8.0
N_KEYS = 128
N_EXPERTS = N_KEYS * N_KEYS
PEER_HEADS = 8
PEER_TOPK = 16
PEER_HALF = 128
N_PAIRS = PEER_HEADS * PEER_TOPK
PLE_DIM = 256
EPS = 1e-6
NEG_INF = -1e30
TAG_NONE = float(1 << 20)
ROUTE_HEADS_PER_STEP = 8

COL_Q, COL_CB, COL_CC, COL_CX, COL_RX, COL_RY, COL_G0 = 0, 1, 2, 3, 4, 5, 6
IN_COLS = 9 * D_MODEL + 2 * KV_W
COL_K_KV = 9 * D_MODEL // KV_W
COL_V_KV = COL_K_KV + 1

LANES = 128
SLAB = D_MODEL // 2 // LANES
ROW_WORDS = SLAB * LANES
TILE_STRIDE = N_PAIRS + 8
INDEX_COLS = 8
SCORES_TOKENS_IN_FLIGHT = 16
COMBINE_TOKENS_IN_FLIGHT = 16
VMEM_LIMIT = 56 * 1024 * 1024


def _params(sem, vmem=None):
    return pltpu.CompilerParams(dimension_semantics=sem, vmem_limit_bytes=vmem)


def _rms(x32, g):
    ms = jnp.mean(x32 * x32, axis=-1, keepdims=True)
    return x32 * lax.rsqrt(ms + EPS) * g


def _sigmoid(x):
    return 0.5 * jnp.tanh(0.5 * x) + 0.5


def _gelu(x):
    return 0.5 * x * (1.0 + jnp.tanh(0.7978845608028654 * (x + 0.044715 * (x * x * x))))


def _split_bf16(x32):
    hi = x32.astype(BF16)
    lo = (x32 - hi.astype(F32)).astype(BF16)
    return hi, lo


def _norm_matmul_kernel(x_ref, g_ref, w_ref, o_ref, xn_ref):
    @pl.when(pl.program_id(1) == 0)
    def _():
        xn_ref[...] = _rms(x_ref[...], g_ref[...]).astype(BF16)

    o_ref[...] = jnp.dot(xn_ref[...], w_ref[...],
                         preferred_element_type=F32).astype(o_ref.dtype)


def norm_matmul(x, g, w, tm=2048, tn=1024):
    m, d = x.shape
    n = w.shape[1]
    tm = min(tm, m)
    return pl.pallas_call(
        _norm_matmul_kernel,
        grid=(m // tm, pl.cdiv(n, tn)),
        in_specs=[pl.BlockSpec((tm, d), lambda i, j: (i, 0)),
                  pl.BlockSpec((1, d), lambda i, j: (0, 0)),
                  pl.BlockSpec((d, tn), lambda i, j: (0, j))],
        out_specs=pl.BlockSpec((tm, tn), lambda i, j: (i, j)),
        out_shape=jax.ShapeDtypeStruct((m, n), BF16),
        scratch_shapes=[pltpu.VMEM((tm, d), BF16)],
        compiler_params=_params(("arbitrary", "arbitrary"), VMEM_LIMIT),
        name="norm_matmul",
    )(x, g, w)


def _attn_block(sink_ref, q, k, v, first):
    shape = (2 * ATTN_BLOCK, GROUP * ATTN_BLOCK)
    ki = lax.broadcasted_iota(jnp.int32, shape, 0)
    col = lax.broadcasted_iota(jnp.int32, shape, 1)
    diff = (col & (ATTN_BLOCK - 1)) + ATTN_BLOCK - ki
    inside = jnp.where(diff >= 0, jnp.where(diff < ATTN_BLOCK, ki - first, -1), -1)
    mask = inside >= 0
    grp = lax.broadcasted_iota(jnp.int32, (1, GROUP * ATTN_BLOCK), 1) // ATTN_BLOCK
    ones = jnp.ones((2 * ATTN_BLOCK, HEAD_DIM), BF16)
    scores, sinks = [], []
    for h in range(N_KV_HEADS):
        kh = k[:, h * HEAD_DIM:(h + 1) * HEAD_DIM]
        qh = jnp.concatenate([q[:, (h * GROUP + g) * HEAD_DIM:(h * GROUP + g + 1) * HEAD_DIM]
                              for g in range(GROUP)], axis=0)
        sink = jnp.zeros((1, GROUP * ATTN_BLOCK), F32)
        for g in range(GROUP):
            sink = jnp.where(grp == g, sink_ref[h * GROUP + g], sink)
        sinks.append(sink)
        scores.append(lax.dot_general(kh, qh, (((1,), (1,)), ((), ())),
                                      preferred_element_type=F32))
    probs, sink_terms = [], []
    for h in range(N_KV_HEADS):
        s = jnp.where(mask, scores[h], NEG_INF)
        m = jnp.maximum(jnp.max(s, axis=0, keepdims=True), sinks[h])
        probs.append(jnp.exp(s - m).astype(BF16))
        sink_terms.append(jnp.exp(sinks[h] - m))
    outs = []
    for h in range(N_KV_HEADS):
        vh = jnp.concatenate([v[:, h * HEAD_DIM:(h + 1) * HEAD_DIM], ones], axis=-1)
        ov = lax.dot_general(vh, probs[h], (((0,), (0,)), ((), ())),
                             preferred_element_type=F32)
        denom = ov[HEAD_DIM:HEAD_DIM + 1, :] + sink_terms[h]
        o = (ov[:HEAD_DIM, :] / denom).T
        outs.extend(o[g * ATTN_BLOCK:(g + 1) * ATTN_BLOCK, :] for g in range(GROUP))
    return jnp.concatenate(outs, axis=-1)


def _attn_kernel(sink_ref, q_ref, kc_ref, vc_ref, kp_ref, vp_ref, o_ref):
    n = pl.program_id(1)
    k = jnp.concatenate([kp_ref[...], kc_ref[...]], axis=0)
    v = jnp.concatenate([vp_ref[...], vc_ref[...]], axis=0)
    for sub in range(ATTN_BLOCKS_PER_STEP):
        rows = slice(sub * ATTN_BLOCK, (sub + 1) * ATTN_BLOCK)
        keys = slice(sub * ATTN_BLOCK, (sub + 2) * ATTN_BLOCK)
        q = q_ref[rows, :] * (HEAD_DIM ** -0.5)
        first = jnp.where(n > 0, 0, ATTN_BLOCK) if sub == 0 else 0
        o_ref[rows, :] = _attn_block(sink_ref, q, k[keys, :], v[keys, :], first).astype(o_ref.dtype)


def attention(z, sinks, batch, seq):
    nb = seq // ATTN_BLOCK
    nsteps = nb // ATTN_BLOCKS_PER_STEP
    rows = ATTN_BLOCKS_PER_STEP * ATTN_BLOCK
    t = batch * seq
    cur = lambda b, n: b * nsteps + n
    prev = lambda b, n: b * nb + jnp.maximum(ATTN_BLOCKS_PER_STEP * n - 1, 0)
    return pl.pallas_call(
        _attn_kernel,
        grid=(batch, nsteps),
        in_specs=[pl.BlockSpec(memory_space=pltpu.SMEM),
                  pl.BlockSpec((rows, D_MODEL), lambda b, n: (cur(b, n), COL_Q)),
                  pl.BlockSpec((rows, KV_W), lambda b, n: (cur(b, n), COL_K_KV)),
                  pl.BlockSpec((rows, KV_W), lambda b, n: (cur(b, n), COL_V_KV)),
                  pl.BlockSpec((ATTN_BLOCK, KV_W), lambda b, n: (prev(b, n), COL_K_KV)),
                  pl.BlockSpec((ATTN_BLOCK, KV_W), lambda b, n: (prev(b, n), COL_V_KV))],
        out_specs=pl.BlockSpec((rows, D_MODEL), lambda b, n: (cur(b, n), 0)),
        out_shape=jax.ShapeDtypeStruct((t, D_MODEL), BF16),
        compiler_params=_params(("arbitrary", "arbitrary")),
        name="swa_attention",
    )(sinks, z, z, z, z, z)


HALO = 8


def _convrec_kernel(cb_ref, cc_ref, cx_ref, rx_ref, ry_ref, cw_ref, rcw_ref, rcb_ref,
                    wr_ref, wi_ref, br_ref, bi_ref, lam_ref, conv_ref, rec_ref,
                    ubuf, xbuf, hcar):
    ts = cb_ref.shape[0]

    @pl.when(pl.program_id(1) == 0)
    def _():
        ubuf[0:HALO, :] = jnp.zeros((HALO, D_MODEL), F32)
        xbuf[0:HALO, :] = jnp.zeros((HALO, D_MODEL), F32)
        hcar[...] = jnp.zeros_like(hcar)

    u = cc_ref[...].astype(F32) * cx_ref[...].astype(F32)
    ubuf[HALO:HALO + ts, :] = u
    y = (cw_ref[2:3, :] * u + cw_ref[1:2, :] * ubuf[HALO - 1:HALO - 1 + ts, :]
         + cw_ref[0:1, :] * ubuf[HALO - 2:HALO - 2 + ts, :])
    conv_ref[...] = (cb_ref[...].astype(F32) * y).astype(conv_ref.dtype)
    ubuf[0:HALO, :] = ubuf[ts:ts + HALO, :]

    x = rx_ref[...].astype(F32)
    xbuf[HALO:HALO + ts, :] = x
    c = (rcw_ref[3:4, :] * x + rcw_ref[2:3, :] * xbuf[HALO - 1:HALO - 1 + ts, :]
         + rcw_ref[1:2, :] * xbuf[HALO - 2:HALO - 2 + ts, :]
         + rcw_ref[0:1, :] * xbuf[HALO - 3:HALO - 3 + ts, :] + rcb_ref[...])
    xbuf[0:HALO, :] = xbuf[ts:ts + HALO, :]
    cbf = c.astype(BF16)
    r_parts, i_parts = [], []
    for h in range(REC_HEADS):
        ch = cbf[:, h * REC_HEAD_DIM:(h + 1) * REC_HEAD_DIM]
        r_parts.append(jnp.dot(ch, wr_ref[h], preferred_element_type=F32))
        i_parts.append(jnp.dot(ch, wi_ref[h], preferred_element_type=F32))
    r = _sigmoid(jnp.concatenate(r_parts, axis=-1) + br_ref[...])
    ig = _sigmoid(jnp.concatenate(i_parts, axis=-1) + bi_ref[...])
    nl = -lam_ref[...]
    softplus = jnp.maximum(nl, 0.0) + jnp.log(1.0 + jnp.exp(-jnp.abs(nl)))
    log_a = -LRU_C * r * softplus
    a = jnp.exp(log_a)
    b = jnp.sqrt(1.0 - jnp.exp(2.0 * log_a)) * (ig * c)
    a3 = a.reshape(ts // 8, 8, D_MODEL)
    b3 = b.reshape(ts // 8, 8, D_MODEL)
    rid = lax.broadcasted_iota(jnp.int32, a3.shape, 1)
    for d in (1, 2, 4):
        inside = rid >= d
        a_sh = jnp.where(inside, pltpu.roll(a3, d, axis=1), 1.0)
        b_sh = jnp.where(inside, pltpu.roll(b3, d, axis=1), 0.0)
        b3 = a3 * b_sh + b3
        a3 = a3 * a_sh
    carry = hcar[0:1, :]
    groups = []
    for g in range(ts // 8):
        hg = b3[g] + a3[g] * carry
        groups.append(hg)
        carry = hg[7:8, :]
    hcar[0:1, :] = carry
    hs = jnp.concatenate(groups, axis=0)
    rec_ref[...] = (_gelu(ry_ref[...].astype(F32)) * hs).astype(rec_ref.dtype)


def conv_rec(z, cw, rcw, rcb, wr, wi, br, bi, lam, batch, seq, ts=512):
    ts = min(ts, seq)
    nt = seq // ts
    t = batch * seq
    row = lambda b, i: b * nt + i
    zspec = lambda col: pl.BlockSpec((ts, D_MODEL), lambda b, i: (row(b, i), col))
    full = lambda a: pl.BlockSpec(a.shape, lambda b, i: (0,) * a.ndim)
    return pl.pallas_call(
        _convrec_kernel,
        grid=(batch, nt),
        in_specs=[zspec(COL_CB), zspec(COL_CC), zspec(COL_CX), zspec(COL_RX), zspec(COL_RY),
                  full(cw), full(rcw), full(rcb), full(wr), full(wi), full(br), full(bi),
                  full(lam)],
        out_specs=[pl.BlockSpec((ts, D_MODEL), lambda b, i: (row(b, i), 0)),
                   pl.BlockSpec((ts, D_MODEL), lambda b, i: (row(b, i), 0))],
        out_shape=[jax.ShapeDtypeStruct((t, D_MODEL), BF16),
                   jax.ShapeDtypeStruct((t, D_MODEL), BF16)],
        scratch_shapes=[pltpu.VMEM((ts + HALO, D_MODEL), F32),
                        pltpu.VMEM((ts + HALO, D_MODEL), F32),
                        pltpu.VMEM((HALO, D_MODEL), F32)],
        compiler_params=_params(("arbitrary", "arbitrary"), VMEM_LIMIT),
        name="conv_rec",
    )(z, z, z, z, z, cw, rcw, rcb, wr, wi, br, bi, lam)


def _merge_kernel(a_ref, c_ref, r_ref, g0_ref, g1_ref, g2_ref, h_ref, wb_ref, wo_ref, o_ref):
    merged = None
    for n, (br, gt) in enumerate(((a_ref, g0_ref), (c_ref, g1_ref), (r_ref, g2_ref))):
        y = jnp.dot(br[...], wb_ref[n], preferred_element_type=F32)
        term = _sigmoid(gt[...].astype(F32)) * y
        merged = term if merged is None else merged + term
    o_ref[...] = h_ref[...] + jnp.dot(merged.astype(BF16), wo_ref[...],
                                      preferred_element_type=F32)


def merge(attn, conv, rec, z, h, wb, wo, tm=512):
    t = h.shape[0]
    tm = min(tm, t)
    blk = lambda col: pl.BlockSpec((tm, D_MODEL), lambda i: (i, col))
    return pl.pallas_call(
        _merge_kernel,
        grid=(t // tm,),
        in_specs=[blk(0), blk(0), blk(0), blk(COL_G0), blk(COL_G0 + 1), blk(COL_G0 + 2), blk(0),
                  pl.BlockSpec(wb.shape, lambda i: (0, 0, 0)),
                  pl.BlockSpec(wo.shape, lambda i: (0, 0))],
        out_specs=blk(0),
        out_shape=jax.ShapeDtypeStruct((t, D_MODEL), F32),
        compiler_params=_params(("arbitrary",), VMEM_LIMIT),
        name="merge",
    )(attn, conv, rec, z, z, z, h, wb, wo)


def _sorting_network(n):
    pairs = []
    p = 1
    while p < n:
        k = p
        while k >= 1:
            for j in range(k % p, n - k, 2 * k):
                for i in range(min(k, n - j - k)):
                    if (i + j) // (2 * p) == (i + j + k) // (2 * p):
                        pairs.append((i + j, i + j + k))
            k //= 2
        p *= 2
    return pairs


def _topk_columns(vals, tags, k):
    vals, tags = list(vals), list(tags)
    n = len(vals)
    for i, j in _sorting_network(n):
        va, ta, vb, tb = vals[i], tags[i], vals[j], tags[j]
        first = (va > vb) | ((va == vb) & (ta < tb))
        vals[i], vals[j] = jnp.where(first, va, vb), jnp.where(first, vb, va)
        tags[i], tags[j] = jnp.where(first, ta, tb), jnp.where(first, tb, ta)
    out_v, out_t = [], []
    for it in range(k):
        m = jnp.max(vals[0], axis=0, keepdims=True)
        best = jnp.min(jnp.where(vals[0] == m, tags[0], TAG_NONE), axis=0, keepdims=True)
        out_v.append(m)
        out_t.append(best)
        left = k - 1 - it
        pop = tags[0] == best
        for d in range(min(left, n - 1)):
            vals[d] = jnp.where(pop, vals[d + 1], vals[d])
            tags[d] = jnp.where(pop, tags[d + 1], tags[d])
        if left > n - 1:
            vals[n - 1] = jnp.where(pop, -jnp.inf, vals[n - 1])
            tags[n - 1] = jnp.where(pop, TAG_NONE, tags[n - 1])
    return out_v, out_t


def _stack_rows(rows, lo, hi):
    n = hi - lo
    rid = lax.broadcasted_iota(jnp.int32, (n, rows[0].shape[1]), 0)
    out = jnp.broadcast_to(rows[lo], (n, rows[0].shape[1]))
    for i in range(1, n):
        out = jnp.where(rid == i, rows[lo + i], out)
    return out


def _pair_candidates(s1, s2):
    k = PEER_TOPK
    tm = s1[0].shape[1]
    r = lax.broadcasted_iota(jnp.int32, (8, tm), 0)
    a_lo, a_hi = _stack_rows(s1, 0, 8), _stack_rows(s1, 8, 16)
    b_lo, b_hi = _stack_rows(s2, 0, 8), _stack_rows(s2, 8, 16)

    def slab(vals, tag, lo, hi):
        keep = jnp.where(r >= lo, r, hi + 1) <= hi
        return jnp.where(keep, vals, -jnp.inf), tag.astype(F32)

    slabs = [slab(a_lo + s2[0], r * k, 0, 7),
             slab(a_hi + s2[0], (r + 8) * k, 0, 7),
             slab(s1[0] + b_lo, r, 1, 7),
             slab(s1[0] + b_hi, r + 8, 0, 7),
             slab(a_lo + s2[1], r * k + 1, 1, 7),
             slab(s1[1] + b_lo, r + k, 2, 7),
             slab(a_lo + s2[2], r * k + 2, 2, 4),
             slab(a_lo + s2[3], r * k + 3, 2, 3),
             slab(a_lo + s2[4], r * k + 4, 2, 2)]
    return [v for v, _ in slabs], [t for _, t in slabs]


def _route_kernel(h_ref, g_ref, wq_ref, keys_ref, xn_ref, ids_ref, gates_ref,
                  q_ref, ids_t, gates_t):
    xn = _rms(h_ref[...], g_ref[...])
    xn_ref[...] = xn
    q_ref[...] = jnp.dot(xn.astype(BF16), wq_ref[...], preferred_element_type=F32)
    tm = h_ref.shape[0]
    row8 = lax.broadcasted_iota(jnp.int32, (8, tm), 0)
    key_tags = [(row8 + 8 * g).astype(F32) for g in range(N_KEYS // 8)]
    key_parts = [_split_bf16(keys_ref[p]) for p in range(2)]

    def one_head(hd):
        tops = []
        for p in range(2):
            c0 = pl.multiple_of((hd * 2 + p) * PEER_HALF, PEER_HALF)
            q_hi, q_lo = _split_bf16(q_ref[:, pl.ds(c0, PEER_HALF)])
            k_hi, k_lo = key_parts[p]
            nt = (((1,), (1,)), ((), ()))
            sc = (lax.dot_general(k_hi, q_hi, nt, preferred_element_type=F32)
                  + lax.dot_general(k_hi, q_lo, nt, preferred_element_type=F32)
                  + lax.dot_general(k_lo, q_hi, nt, preferred_element_type=F32))
            slabs = [sc[8 * g:8 * g + 8, :] for g in range(N_KEYS // 8)]
            tops.append(_topk_columns(slabs, key_tags, PEER_TOPK))
        (s1, i1), (s2, i2) = tops
        cand, flat = _pair_candidates(s1, s2)
        best_s, best_c = _topk_columns(cand, flat, PEER_TOPK)
        best_s = _stack_rows(best_s, 0, PEER_TOPK)
        flat_c = _stack_rows(best_c, 0, PEER_TOPK).astype(jnp.int32)
        a_idx = flat_c >> 4
        b_idx = flat_c & (PEER_TOPK - 1)
        e1 = jnp.zeros(flat_c.shape, F32)
        e2 = jnp.zeros(flat_c.shape, F32)
        for a in range(PEER_TOPK):
            e1 = jnp.where(a_idx == a, i1[a], e1)
            e2 = jnp.where(b_idx == a, i2[a], e2)
        ex = jnp.exp(best_s - best_s[0:1, :])
        r0 = pl.multiple_of(hd * PEER_TOPK, PEER_TOPK)
        gates_t[pl.ds(r0, PEER_TOPK), :] = ex / jnp.sum(ex, axis=0, keepdims=True)
        ids_t[pl.ds(r0, PEER_TOPK), :] = ((e1 * N_KEYS + e2) * SLAB).astype(jnp.int32)

    def heads(step, carry):
        for i in range(ROUTE_HEADS_PER_STEP):
            one_head(step * ROUTE_HEADS_PER_STEP + i)
        return carry

    lax.fori_loop(0, PEER_HEADS // ROUTE_HEADS_PER_STEP, heads, 0)
    ids_ref[...] = ids_t[...].T
    gates_ref[...] = gates_t[...].T


def route(h, g, wq, keys, tm=128):
    t = h.shape[0]
    return pl.pallas_call(
        _route_kernel,
        grid=(t // tm,),
        in_specs=[pl.BlockSpec((tm, D_MODEL), lambda i: (i, 0)),
                  pl.BlockSpec((1, D_MODEL), lambda i: (0, 0)),
                  pl.BlockSpec(wq.shape, lambda i: (0, 0)),
                  pl.BlockSpec(keys.shape, lambda i: (0, 0, 0))],
        out_specs=[pl.BlockSpec((tm, D_MODEL), lambda i: (i, 0)),
                   pl.BlockSpec((tm, N_PAIRS), lambda i: (i, 0)),
                   pl.BlockSpec((tm, N_PAIRS), lambda i: (i, 0))],
        out_shape=[jax.ShapeDtypeStruct((t, D_MODEL), F32),
                   jax.ShapeDtypeStruct((t, N_PAIRS), jnp.int32),
                   jax.ShapeDtypeStruct((t, N_PAIRS), F32)],
        scratch_shapes=[pltpu.VMEM((tm, 2 * PEER_HALF * PEER_HEADS), F32),
                        pltpu.VMEM((N_PAIRS, tm), jnp.int32),
                        pltpu.VMEM((N_PAIRS, tm), F32)],
        compiler_params=_params(("arbitrary",), VMEM_LIMIT),
        name="peer_route",
    )(h, g, wq, keys)


def _gather_tokens(ids_ref, tab_ref, tiles, t0):
    zero = lax.min(t0, 0)
    offs = [zero + u for u in range(INDEX_COLS)]
    for jj in range(N_PAIRS // INDEX_COLS):
        rows = [ids_ref.at[t0 + q, pl.ds(INDEX_COLS * jj, INDEX_COLS)] for q in range(len(tiles))]
        for u in range(INDEX_COLS):
            j = INDEX_COLS * jj + u
            for q, tile in enumerate(tiles):
                e = pl.multiple_of(rows[q][offs[u]], SLAB)
                tile[pl.ds(j, SLAB, stride=TILE_STRIDE), :] = tab_ref[pl.ds(e, SLAB), :]


def _pipelined_tokens(tb, tiles, consume, ids_ref, tab_ref):
    nfl = len(tiles)

    def body(k, carry):
        t0 = nfl * k
        for q in range(nfl):
            consume(jnp.maximum(t0 - nfl + q, 0), tiles[q])
        _gather_tokens(ids_ref, tab_ref, tiles, t0)
        return carry

    lax.fori_loop(0, tb // nfl, body, 0)
    for q in range(nfl):
        consume(tb - nfl + q, tiles[q])


def _tile_chunk(tile, s):
    return pltpu.bitcast(tile[pl.ds(s * TILE_STRIDE, N_PAIRS), :], BF16)


def _lane_parity(shape):
    lane = lax.broadcasted_iota(jnp.int32, shape, len(shape) - 1)
    return (lane & 1) == 1


def _peer_u_kernel(ids_ref, x_ref, gates_ref, tab_ref, w_ref, sc_ref, *tiles):
    tb = x_ref.shape[0]

    @pl.when(pl.program_id(0) == 0)
    def _():
        for tile in tiles:
            tile[...] = jnp.zeros_like(tile)

    def scores(t, tile):
        g = jnp.concatenate([_tile_chunk(tile, s) for s in range(SLAB)], axis=-1)
        xr = x_ref[pl.ds(t, 1), :]
        xa_hi, xa_lo = _split_bf16(xr[:, :ROW_WORDS])
        xb_hi, xb_lo = _split_bf16(xr[:, ROW_WORDS:])
        x8 = jnp.concatenate([xa_hi, xb_hi, xa_lo, xb_lo,
                              jnp.zeros((4, ROW_WORDS), BF16)], axis=0)
        res = lax.dot_general(x8, g, (((1,), (1,)), ((), ())),
                              preferred_element_type=F32)
        odd = _lane_parity((1, 2 * N_PAIRS))
        part = jnp.where(odd, res[1:2] + res[3:4], res[0:1] + res[2:3])
        sc_ref[pl.ds(t, 1), :] = part + pltpu.roll(part, 1, axis=1)

    _pipelined_tokens(tb, tiles, scores, ids_ref, tab_ref)

    r = lax.broadcasted_iota(jnp.int32, (N_PAIRS, 2 * N_PAIRS), 0)
    c = lax.broadcasted_iota(jnp.int32, (N_PAIRS, 2 * N_PAIRS), 1)
    spread = jnp.where(c == 2 * r + 1, 1.0, 0.0).astype(BF16)
    g_hi, g_lo = _split_bf16(gates_ref[...])
    g2 = (jnp.dot(g_hi, spread, preferred_element_type=F32)
          + jnp.dot(g_lo, spread, preferred_element_type=F32))
    w_ref[...] = g2 * _gelu(sc_ref[...])


def _table_spec():
    return pl.BlockSpec((N_EXPERTS * SLAB, LANES), lambda i: (0, 0),
                        pipeline_mode=pl.Buffered(1))


def _tile_scratch(n):
    return [pltpu.VMEM((SLAB * TILE_STRIDE, LANES), jnp.int32) for _ in range(n)]


def peer_scores(ids, xn, gates, tab, tb=512):
    t = xn.shape[0]
    return pl.pallas_call(
        _peer_u_kernel,
        grid=(t // tb,),
        in_specs=[pl.BlockSpec((tb, N_PAIRS), lambda i: (i, 0), memory_space=pltpu.SMEM),
                  pl.BlockSpec((tb, D_MODEL), lambda i: (i, 0)),
                  pl.BlockSpec((tb, N_PAIRS), lambda i: (i, 0)),
                  _table_spec()],
        out_specs=pl.BlockSpec((tb, 2 * N_PAIRS), lambda i: (i, 0)),
        out_shape=jax.ShapeDtypeStruct((t, 2 * N_PAIRS), F32),
        scratch_shapes=([pltpu.VMEM((tb, 2 * N_PAIRS), F32)]
                        + _tile_scratch(SCORES_TOKENS_IN_FLIGHT)),
        compiler_params=_params(("arbitrary",), VMEM_LIMIT),
        name="peer_u",
    )(ids, xn, gates, tab)


def _peer_v_kernel(ids_ref, w_ref, h_ref, tab_ref, o_ref, acc_ref, lhs_refs, *tiles):
    tb = h_ref.shape[0]

    @pl.when(pl.program_id(0) == 0)
    def _():
        for tile in tiles:
            tile[...] = jnp.zeros_like(tile)

    w_odd = w_ref[...]
    w_even = pltpu.roll(w_odd, 2 * N_PAIRS - 1, axis=1)
    for i, w in enumerate((w_even, w_odd)):
        hi = w.astype(BF16).astype(F32)
        lhs_refs[i, :, :] = hi
        lhs_refs[2 + i, :, :] = w - hi

    def combine(t, tile):
        rid = lax.broadcasted_iota(jnp.int32, (8, 2 * N_PAIRS), 0)
        lhs = jnp.zeros((8, 2 * N_PAIRS), F32)
        for i in range(4):
            row = lhs_refs[i, pl.ds(t, 1), :]
            lhs = jnp.where(rid == i, row, lhs)
        lhs = lhs.astype(BF16)
        lo, hi = [], []
        for s in range(SLAB):
            res = jnp.dot(lhs, _tile_chunk(tile, s), preferred_element_type=F32)
            lo.append(res[0:1] + res[2:3])
            hi.append(res[1:2] + res[3:4])
        acc_ref[pl.ds(t, 1), :] = jnp.concatenate(lo + hi, axis=-1)

    _pipelined_tokens(tb, tiles, combine, ids_ref, tab_ref)
    o_ref[...] = h_ref[...] + acc_ref[...]


def peer_combine(ids, w, h, tab, tb=512):
    t = h.shape[0]
    return pl.pallas_call(
        _peer_v_kernel,
        grid=(t // tb,),
        in_specs=[pl.BlockSpec((tb, N_PAIRS), lambda i: (i, 0), memory_space=pltpu.SMEM),
                  pl.BlockSpec((tb, 2 * N_PAIRS), lambda i: (i, 0)),
                  pl.BlockSpec((tb, D_MODEL), lambda i: (i, 0)),
                  _table_spec()],
        out_specs=pl.BlockSpec((tb, D_MODEL), lambda i: (i, 0)),
        out_shape=jax.ShapeDtypeStruct((t, D_MODEL), F32),
        scratch_shapes=[pltpu.VMEM((tb, D_MODEL), F32),
                        pltpu.VMEM((4, tb, 2 * N_PAIRS), F32)]
                       + _tile_scratch(COMBINE_TOKENS_IN_FLIGHT),
        compiler_params=_params(("arbitrary",), VMEM_LIMIT),
        name="peer_v",
    )(ids, w, h, tab)


def _ple_kernel(h_ref, p_ref, g_ref, wg_ref, wp_ref, gf_ref, o_ref, *, final):
    h = h_ref[...]
    gate = _sigmoid(jnp.dot(_rms(h, g_ref[...]).astype(BF16), wg_ref[...],
                            preferred_element_type=F32))
    out = h + gate * jnp.dot(p_ref[...].astype(BF16), wp_ref[...],
                             preferred_element_type=F32)
    if final:
        out = _rms(out, gf_ref[...])
    o_ref[...] = out


def ple(h, p, layer, g, wg, wp, gf, final, tm=512):
    t = h.shape[0]
    tm = min(tm, t)
    return pl.pallas_call(
        functools.partial(_ple_kernel, final=final),
        grid=(t // tm,),
        in_specs=[pl.BlockSpec((tm, D_MODEL), lambda i: (i, 0)),
                  pl.BlockSpec((None, tm, PLE_DIM), lambda i: (layer, i, 0)),
                  pl.BlockSpec((1, D_MODEL), lambda i: (0, 0)),
                  pl.BlockSpec(wg.shape, lambda i: (0, 0)),
                  pl.BlockSpec(wp.shape, lambda i: (0, 0)),
                  pl.BlockSpec((1, D_MODEL), lambda i: (0, 0))],
        out_specs=pl.BlockSpec((tm, D_MODEL), lambda i: (i, 0)),
        out_shape=jax.ShapeDtypeStruct((t, D_MODEL), F32),
        compiler_params=_params(("arbitrary",), VMEM_LIMIT),
        name="ple",
    )(h, p, g, wg, wp, gf)


def _cast_kernel(w_ref, o_ref):
    o_ref[...] = w_ref[...].astype(o_ref.dtype)


def _permute_w_in(w_in, layer, tn=2 * KV_W):
    nb = IN_COLS // tn
    q_blocks, kv_blocks = D_MODEL // tn, 2 * KV_W // tn

    def src(j):
        return jnp.where(j < q_blocks, j,
                         jnp.where(j < nb - kv_blocks, j + kv_blocks, j - (nb - kv_blocks) + q_blocks))

    return pl.pallas_call(
        _cast_kernel,
        grid=(nb,),
        in_specs=[pl.BlockSpec((None, D_MODEL, tn), lambda j: (layer, 0, src(j)))],
        out_specs=pl.BlockSpec((D_MODEL, tn), lambda j: (0, j)),
        out_shape=jax.ShapeDtypeStruct((D_MODEL, IN_COLS), BF16),
        compiler_params=_params(("arbitrary",)),
        name="cast_w_in",
    )(w_in)


def _bf16_high_bits(x):
    bits = lax.bitcast_convert_type(x, jnp.uint32)
    return bits + (jnp.uint32(0x7FFF) + ((bits >> 16) & jnp.uint32(1)))


def _pack_kernel(t_ref, o_ref):
    x = t_ref[...]
    rows = x.shape[0]
    for s in range(SLAB):
        lo = _bf16_high_bits(x[:, s * LANES:(s + 1) * LANES])
        hi = _bf16_high_bits(x[:, ROW_WORDS + s * LANES:ROW_WORDS + (s + 1) * LANES])
        words = (lo >> 16) | (hi & jnp.uint32(0xFFFF0000))
        o_ref[pl.ds(s, rows, stride=SLAB), :] = lax.bitcast_convert_type(words, jnp.int32)


def _pack_table(tabs, layer, te=512):
    return pl.pallas_call(
        _pack_kernel,
        grid=(N_EXPERTS // te,),
        in_specs=[pl.BlockSpec((None, te, D_MODEL), lambda i: (layer, i, 0))],
        out_specs=pl.BlockSpec((te * SLAB, LANES), lambda i: (i, 0)),
        out_shape=jax.ShapeDtypeStruct((N_EXPERTS * SLAB, LANES), jnp.int32),
        compiler_params=_params(("arbitrary",)),
        name="pack_table",
    )(tabs)


def kernel(x, p, norm_mix, w_in, attn_sinks, conv_w, rec_conv_w, rec_conv_b, w_rgate, b_rgate, w_igate, b_igate, lru_lambda, w_branch, w_out, norm_ffn, w_peer_q, peer_sub_keys, peer_u, peer_v, norm_ple, w_ple_gate, w_ple_proj, norm_final):
    batch, seq, d = x.shape
    depth = w_in.shape[0]
    t = batch * seq
    row = lambda a: a.reshape(1, -1)
    h = x.reshape(t, d)
    p3 = p.reshape(depth, t, -1)
    for l in range(depth):
        z = norm_matmul(h, row(norm_mix[l]), _permute_w_in(w_in, l))
        attn = attention(z, attn_sinks[l], batch, seq)
        conv, rec = conv_rec(z, conv_w[l], rec_conv_w[l], row(rec_conv_b[l]),
                             w_rgate[l].astype(BF16), w_igate[l].astype(BF16),
                             row(b_rgate[l]), row(b_igate[l]), row(lru_lambda[l]), batch, seq)
        h = merge(attn, conv, rec, z, h, w_branch[l].astype(BF16), w_out[l].astype(BF16))
        xn, ids, gates = route(h, row(norm_ffn[l]), w_peer_q[l].astype(BF16), peer_sub_keys[l])
        w = peer_scores(ids, xn, gates, _pack_table(peer_u, l))
        h = peer_combine(ids, w, h, _pack_table(peer_v, l))
        h = ple(h, p3, l, row(norm_ple[l]), w_ple_gate[l].astype(BF16),
                w_ple_proj[l].astype(BF16), row(norm_final), final=(l == depth - 1))
    return h.reshape(batch, seq, d)
```
